```python
import math
import jax
import jax.numpy as jnp
from jax import lax
import numpy as np

D_MODEL = 4096
BATCH = 1
SEQ = 8192
DEPTH = 2

ATT_HEADS = 16
ATT_HEAD_DIM = 128
ATT_WIDTH = ATT_HEADS * ATT_HEAD_DIM
DILATED_BRANCHES = ((128, 1), (512, 4), (2048, 16))
ROPE_THETA = 500000.0
ROPE_DIM = ATT_HEAD_DIM // 4
SSD_HEADS = 32
SSD_HEAD_DIM = 64
SSD_WIDTH = SSD_HEADS * SSD_HEAD_DIM
SSD_GROUPS = 8
SSD_STATE = 128
SSD_CONV = 4
SSD_CHUNK = 128
SSD_CONV_CH = SSD_WIDTH + 2 * SSD_GROUPS * SSD_STATE
EVEN_IN = 3 * ATT_WIDTH + SSD_WIDTH + SSD_CONV_CH + SSD_HEADS
EVEN_OUT = ATT_WIDTH + SSD_WIDTH
RWKV_HEADS = 32
RWKV_HEAD_DIM = 64
RWKV_WIDTH = RWKV_HEADS * RWKV_HEAD_DIM
DECAY_RANK = 96
ICL_RANK = 96
GATE_RANK = 256
RWKV_IN = 3 * RWKV_WIDTH + DECAY_RANK + ICL_RANK + GATE_RANK
RWKV_GN_EPS = 64e-5
S5_GROUP = 16
S5_GROUPS = 128
S5_WIDTH = S5_GROUP * S5_GROUPS
S5_STATE = 64
ODD_IN = RWKV_IN + S5_WIDTH
ODD_OUT = RWKV_WIDTH + S5_WIDTH
N_EXPERT_GROUPS = 4
EXPERTS_PER_GROUP = 8
N_EXPERTS = N_EXPERT_GROUPS * EXPERTS_PER_GROUP
TOP_K = 2
EXPERT_FF = 1024
MOE_BLOCK = 128
NORM_EPS = 1e-6

kernel_name = 'hybrid_dilated_ssd_rwkv7_s5_hmoe'


def _split(t, sizes):
    return jnp.split(t, [int(v) for v in np.cumsum(sizes)[:-1]], axis=-1)


def rms_norm(x, w, eps=NORM_EPS):
    xf = x.astype(jnp.float32)
    y = xf * lax.rsqrt(jnp.mean(xf * xf, axis=-1, keepdims=True) + eps)
    return (y * w.astype(jnp.float32)).astype(x.dtype)


def partial_rotary(x, pos):
    half = ROPE_DIM // 2
    inv_freq = ROPE_THETA ** (-jnp.arange(half, dtype=jnp.float32) / half)
    ang = pos.astype(jnp.float32)[:, None] * inv_freq[None, :]
    cos = jnp.cos(ang)[None, :, None, :]
    sin = jnp.sin(ang)[None, :, None, :]
    xf = x.astype(jnp.float32)
    x1, x2, rest = xf[..., :half], xf[..., half:ROPE_DIM], xf[..., ROPE_DIM:]
    return jnp.concatenate([x1 * cos - x2 * sin, x2 * cos + x1 * sin, rest], axis=-1)


def dilated_branch(q, k, v, window, dilation):
    B, S, H, Dh = q.shape
    W = window // dilation
    L = S // dilation
    nb = -(-L // W)
    Lp = nb * W

    def to_blocks(t):
        t = t.reshape(B, L, dilation, H, Dh).transpose(0, 2, 1, 3, 4)
        t = jnp.pad(t, ((0, 0), (0, 0), (0, Lp - L), (0, 0), (0, 0)))
        return t.reshape(B, dilation, nb, W, H, Dh)

    def with_prev(t):
        prev = jnp.pad(t, ((0, 0), (0, 0), (1, 0), (0, 0), (0, 0), (0, 0)))[:, :, :-1]
        return jnp.concatenate([prev, t], axis=3)

    qb = to_blocks(q)
    kw = with_prev(to_blocks(k))
    vw = with_prev(to_blocks(v))
    s = jnp.einsum('bdnqhc,bdnkhc->bdnhqk', qb, kw) * (Dh ** -0.5)
    qi = jnp.arange(W)[:, None]
    kj = jnp.arange(2 * W)[None, :]
    dist = W + qi - kj
    band = (dist >= 0) & (dist <= W)
    key_sub = (jnp.arange(nb)[:, None, None] - 1) * W + kj[None]
    valid = band[None] & (key_sub >= 0)
    s = jnp.where(valid[None, None, :, None], s, -jnp.inf)
    m = jnp.max(s, axis=-1, keepdims=True)
    p = jnp.exp(s - m)
    l = jnp.sum(p, axis=-1, keepdims=True)
    o = jnp.einsum('bdnhqk,bdnkhc->bdnqhc', p / l, vw)
    lse = jnp.swapaxes((m + jnp.log(l))[..., 0], 3, 4)
    o = o.reshape(B, dilation, Lp, H, Dh)[:, :, :L].transpose(0, 2, 1, 3, 4).reshape(B, S, H, Dh)
    lse = lse.reshape(B, dilation, Lp, H)[:, :, :L].transpose(0, 2, 1, 3).reshape(B, S, H)
    return o, lse


def dilated_attention(q, k, v, q_norm_w, k_norm_w, pos):
    B, S, H, Dh = q.shape
    q = partial_rotary(rms_norm(q, q_norm_w), pos)
    k = partial_rotary(rms_norm(k, k_norm_w), pos)
    v = v.astype(jnp.float32)
    outs, lses = [], []
    for window, dilation in DILATED_BRANCHES:
        o, lse = dilated_branch(q, k, v, window, dilation)
        outs.append(o)
        lses.append(lse)
    wts = jax.nn.softmax(jnp.stack(lses, axis=0), axis=0)
    o = jnp.einsum('rbsh,rbshc->bshc', wts, jnp.stack(outs, axis=0))
    return o.reshape(B, S, H * Dh)


def causal_depthwise_conv(x, w, b):
    K, C = w.shape
    y = lax.conv_general_dilated(x, w[:, None, :].astype(x.dtype), window_strides=(1,),
                                 padding=[(K - 1, 0)], dimension_numbers=('NWC', 'WIO', 'NWC'),
                                 feature_group_count=C)
    return y + b.astype(x.dtype)


def ssd_scan(xh, dt, a, bm, cm):
    B, S, H, P = xh.shape
    G, N = bm.shape[2], bm.shape[3]
    E = H // G
    Q = SSD_CHUNK
    NC = S // Q
    x = (xh.astype(jnp.float32) * dt[..., None]).reshape(B, NC, Q, G, E, P)
    acs = jnp.cumsum((dt * a).reshape(B, NC, Q, G, E), axis=2)
    bm = bm.astype(jnp.float32).reshape(B, NC, Q, G, N)
    cm = cm.astype(jnp.float32).reshape(B, NC, Q, G, N)
    seg = acs[:, :, :, None] - acs[:, :, None, :]
    causal = (jnp.arange(Q)[:, None] >= jnp.arange(Q)[None, :])[:, :, None, None]
    decay = jnp.exp(jnp.where(causal, seg, -jnp.inf))
    cb = jnp.einsum('bclgn,bcsgn->bclsg', cm, bm)
    y_diag = jnp.einsum('bclsg,bclsge,bcsgep->bclgep', cb, decay, x)
    decay_to_end = jnp.exp(acs[:, :, -1:] - acs)
    chunk_states = jnp.einsum('bclgn,bclge,bclgep->bcgepn', bm, decay_to_end, x)
    chunk_decay = jnp.exp(acs[:, :, -1])

    def carry(state, inp):
        st, dec = inp
        return state * dec[..., None, None] + st, state

    h0 = jnp.zeros((B, G, E, P, N), jnp.float32)
    _, h_in = lax.scan(carry, h0, (jnp.moveaxis(chunk_states, 1, 0), jnp.moveaxis(chunk_decay, 1, 0)))
    h_in = jnp.moveaxis(h_in, 0, 1)
    y_off = jnp.einsum('bclgn,bcgepn,bclge->bclgep', cm, h_in, jnp.exp(acs))
    return (y_diag + y_off).reshape(B, S, H, P)


def mamba2_mixer(z, xbc, dt_raw, conv_w, conv_b, dt_bias, a_log, d_skip, norm_w):
    B, S, _ = z.shape
    xbc = jax.nn.silu(causal_depthwise_conv(xbc, conv_w, conv_b))
    xs, bm, cm = _split(xbc, [SSD_WIDTH, SSD_GROUPS * SSD_STATE, SSD_GROUPS * SSD_STATE])
    xh = xs.reshape(B, S, SSD_HEADS, SSD_HEAD_DIM)
    dt = jax.nn.softplus(dt_raw.astype(jnp.float32) + dt_bias.astype(jnp.float32))
    a = -jnp.exp(a_log.astype(jnp.float32))
    y = ssd_scan(xh, dt, a, bm.reshape(B, S, SSD_GROUPS, SSD_STATE), cm.reshape(B, S, SSD_GROUPS, SSD_STATE))
    y = y + d_skip.astype(jnp.float32)[:, None] * xh.astype(jnp.float32)
    y = y.reshape(B, S, SSD_WIDTH) * jax.nn.silu(z.astype(jnp.float32))
    return rms_norm(y, norm_w)


def even_mixer(xn, pos, w_in, q_norm_w, k_norm_w, conv_w, conv_b, dt_bias, a_log, d_skip, ssd_norm_w, w_out):
    B, S, _ = xn.shape
    proj = xn @ w_in
    q, k, v, z, xbc, dt_raw = _split(proj, [ATT_WIDTH, ATT_WIDTH, ATT_WIDTH, SSD_WIDTH, SSD_CONV_CH, SSD_HEADS])
    heads = lambda t: t.reshape(B, S, ATT_HEADS, ATT_HEAD_DIM)
    y_att = dilated_attention(heads(q), heads(k), heads(v), q_norm_w, k_norm_w, pos)
    y_ssd = mamba2_mixer(z, xbc, dt_raw, conv_w, conv_b, dt_bias, a_log, d_skip, ssd_norm_w)
    return jnp.concatenate([y_att.astype(xn.dtype), y_ssd.astype(xn.dtype)], axis=-1) @ w_out


def rwkv7_scan(r, w, k, v, a, b):
    B, S, H, N = r.shape

    def step(state, inp):
        r_t, w_t, k_t, v_t, a_t, b_t = inp
        sa = jnp.einsum('bhvk,bhk->bhv', state, a_t)
        state = state * w_t[:, :, None, :] + sa[..., None] * b_t[:, :, None, :] + v_t[..., None] * k_t[:, :, None, :]
        return state, jnp.einsum('bhvk,bhk->bhv', state, r_t)

    xs = (jnp.moveaxis(r, 1, 0), jnp.moveaxis(w, 1, 0), jnp.moveaxis(k, 1, 0),
          jnp.moveaxis(v, 1, 0), jnp.moveaxis(a, 1, 0), jnp.moveaxis(b, 1, 0))
    _, y = lax.scan(step, jnp.zeros((B, H, N, N), jnp.float32), xs)
    return jnp.moveaxis(y, 0, 1)


def rwkv7_time_mix(pc, mu, w0, w2, a0, a2, g2, k_k, k_a, r_k, lnx_w, lnx_b):
    B, S, _ = pc.shape
    H, N = RWKV_HEADS, RWKV_HEAD_DIM
    prev = jnp.pad(pc, ((0, 0), (1, 0), (0, 0)))[:, :-1]
    p = (pc + (prev - pc) * mu).astype(jnp.float32)
    r, k, v, wd, ad, gd = _split(p, [RWKV_WIDTH, RWKV_WIDTH, RWKV_WIDTH, DECAY_RANK, ICL_RANK, GATE_RANK])
    w_log = -jax.nn.softplus(-(w0 + jnp.tanh(wd) @ w2)) - 0.5
    decay = jnp.exp(-jnp.exp(w_log))
    a = jax.nn.sigmoid(a0 + ad @ a2)
    g = jax.nn.sigmoid(gd) @ g2
    heads = lambda t: t.reshape(B, S, H, N)
    kk = heads(k * k_k)
    kk = kk / jnp.maximum(jnp.linalg.norm(kk, axis=-1, keepdims=True), 1e-12)
    k = k * (1.0 + (a - 1.0) * k_a)
    r, k, v, decay, a = heads(r), heads(k), heads(v), heads(decay), heads(a)
    y = rwkv7_scan(r, decay, k, v, -kk, kk * a)
    mean = jnp.mean(y, axis=-1, keepdims=True)
    var = jnp.mean(jnp.square(y - mean), axis=-1, keepdims=True)
    y = ((y - mean) * lax.rsqrt(var + RWKV_GN_EPS)).reshape(B, S, RWKV_WIDTH) * lnx_w + lnx_b
    bonus = jnp.sum(r * k * r_k, axis=-1, keepdims=True) * v
    return (y + bonus.reshape(B, S, RWKV_WIDTH)) * g


def s5_glu(u, a_re, a_im, log_step, b_re, b_im, c_re, c_im, d_skip, glu_w, glu_b):
    B, S, _ = u.shape
    f32 = jnp.float32
    uf = u.astype(f32).reshape(B, S, S5_GROUPS, S5_GROUP)
    lam_re = jnp.minimum(a_re.astype(f32), -1e-4)
    lam_im = a_im.astype(f32)
    step = jnp.exp(log_step.astype(f32))[:, None]
    mag = jnp.exp(lam_re * step)
    lb_re = mag * jnp.cos(lam_im * step)
    lb_im = mag * jnp.sin(lam_im * step)
    den = lam_re * lam_re + lam_im * lam_im
    f_re = ((lb_re - 1.0) * lam_re + lb_im * lam_im) / den
    f_im = (lb_im * lam_re - (lb_re - 1.0) * lam_im) / den
    b_re = b_re.astype(f32)
    b_im = b_im.astype(f32)
    bb_re = f_re[..., None] * b_re - f_im[..., None] * b_im
    bb_im = f_re[..., None] * b_im + f_im[..., None] * b_re
    bu_re = jnp.einsum('bsgh,gph->bsgp', uf, bb_re)
    bu_im = jnp.einsum('bsgh,gph->bsgp', uf, bb_im)
    ar = jnp.broadcast_to(lb_re, bu_re.shape)
    ai = jnp.broadcast_to(lb_im, bu_re.shape)

    def combine(left, right):
        ar1, ai1, br1, bi1 = left
        ar2, ai2, br2, bi2 = right
        return (ar2 * ar1 - ai2 * ai1, ar2 * ai1 + ai2 * ar1,
                ar2 * br1 - ai2 * bi1 + br2, ar2 * bi1 + ai2 * br1 + bi2)

    _, _, xr, xi = lax.associative_scan(combine, (ar, ai, bu_re, bu_im), axis=1)
    y = (jnp.einsum('bsgp,ghp->bsgh', xr, c_re.astype(f32))
         - jnp.einsum('bsgp,ghp->bsgh', xi, c_im.astype(f32))
         + d_skip.astype(f32) * uf)
    y = jax.nn.gelu(y.reshape(B, S, S5_WIDTH))
    return y * jax.nn.sigmoid(y @ glu_w + glu_b)


def odd_mixer(xn, w_in, mu, w0, w2, a0, a2, g2, k_k, k_a, r_k, lnx_w, lnx_b,
              s5_a_re, s5_a_im, s5_log_step, s5_b_re, s5_b_im, s5_c_re, s5_c_im, s5_d,
              glu_w, glu_b, w_out):
    proj = xn @ w_in
    pc, u = proj[..., :RWKV_IN], proj[..., RWKV_IN:]
    y_c = rwkv7_time_mix(pc, mu, w0, w2, a0, a2, g2, k_k, k_a, r_k, lnx_w, lnx_b)
    y_d = s5_glu(u, s5_a_re, s5_a_im, s5_log_step, s5_b_re, s5_b_im, s5_c_re, s5_c_im, s5_d, glu_w, glu_b)
    return jnp.concatenate([y_c.astype(xn.dtype), y_d.astype(xn.dtype)], axis=-1) @ w_out


def grouped_experts(x, expert_idx, combine, w1, w3, w2):
    T, D = x.shape
    E = w1.shape[0]
    BLK = MOE_BLOCK
    A = T * TOP_K
    flat_e = expert_idx.reshape(-1)
    flat_tok = jnp.repeat(jnp.arange(T, dtype=jnp.int32), TOP_K)
    flat_w = combine.reshape(-1)
    order = jnp.argsort(flat_e)
    se, stok, sw = flat_e[order], flat_tok[order], flat_w[order]
    counts = jnp.bincount(flat_e, length=E)
    padded = (counts + BLK - 1) // BLK * BLK
    pad_end = jnp.cumsum(padded)
    pad_start = pad_end - padded
    start = jnp.cumsum(counts) - counts
    dest = pad_start[se] + jnp.arange(A) - start[se]
    n_blocks = -(-A // BLK) + E
    slot_tok = jnp.zeros((n_blocks * BLK,), jnp.int32).at[dest].set(stok)
    slot_w = jnp.zeros((n_blocks * BLK,), jnp.float32).at[dest].set(sw)
    block_e = jnp.minimum(jnp.searchsorted(pad_end, jnp.arange(n_blocks) * BLK, side='right'), E - 1)

    def expert_block(args):
        tok, e = args
        xb = x[tok]
        hb = jax.nn.silu(xb @ w1[e]) * (xb @ w3[e])
        return hb @ w2[e]

    yb = lax.map(expert_block, (slot_tok.reshape(n_blocks, BLK), block_e))
    y = jnp.zeros((T, D), jnp.float32).at[slot_tok].add(yb.reshape(-1, D).astype(jnp.float32) * slot_w[:, None])
    return y


def hierarchical_moe(h, wg, bg, we, be, w1, w3, w2):
    B, S, D = h.shape
    x = h.reshape(-1, D)
    T = x.shape[0]
    pg = jax.nn.softmax((x @ wg).astype(jnp.float32) + bg.astype(jnp.float32), axis=-1)
    pg_top, g_sel = lax.top_k(pg, 1)
    le = ((x @ we).astype(jnp.float32) + be.astype(jnp.float32)).reshape(T, N_EXPERT_GROUPS, EXPERTS_PER_GROUP)
    idx = jnp.broadcast_to(g_sel[:, :, None], (T, 1, EXPERTS_PER_GROUP))
    le = jnp.take_along_axis(le, idx, axis=1)[:, 0]
    pe_top, e_sel = lax.top_k(jax.nn.softmax(le, axis=-1), TOP_K)
    combine = pg_top * pe_top / jnp.sum(pe_top, axis=-1, keepdims=True)
    expert_idx = g_sel * EXPERTS_PER_GROUP + e_sel
    y = grouped_experts(x, expert_idx, combine, w1, w3, w2)
    return y.reshape(B, S, D)


def setup_inputs(seed: int = 0) -> dict:
    key = jax.random.key(seed)
    ks = iter(jax.random.split(key, 64))
    f32 = jnp.float32
    NE = (DEPTH + 1) // 2
    NO = DEPTH // 2

    def nrm(shape, scale):
        return jax.random.normal(next(ks), shape, f32) * scale

    def gain(shape):
        return 1.0 + 0.02 * jax.random.normal(next(ks), shape, f32)

    def unif(shape, lo, hi):
        return jax.random.uniform(next(ks), shape, f32, lo, hi)

    dt0 = jnp.exp(unif((NE, SSD_HEADS), math.log(1e-3), math.log(1e-1)))
    return {
        'x': nrm((BATCH, SEQ, D_MODEL), 1.0),
        'norm_mix_w': gain((DEPTH, D_MODEL)),
        'norm_ffn_w': gain((DEPTH, D_MODEL)),
        'ev_w_in': nrm((NE, D_MODEL, EVEN_IN), D_MODEL ** -0.5),
        'ev_q_norm_w': gain((NE, ATT_HEAD_DIM)),
        'ev_k_norm_w': gain((NE, ATT_HEAD_DIM)),
        'ev_conv_w': nrm((NE, SSD_CONV, SSD_CONV_CH), SSD_CONV ** -0.5),
        'ev_conv_b': nrm((NE, SSD_CONV_CH), 0.02),
        'ev_dt_bias': dt0 + jnp.log(-jnp.expm1(-dt0)),
        'ev_a_log': jnp.log(unif((NE, SSD_HEADS), 1.0, 16.0)),
        'ev_d_skip': gain((NE, SSD_HEADS)),
        'ev_ssd_norm_w': gain((NE, SSD_WIDTH)),
        'ev_w_out': nrm((NE, EVEN_OUT, D_MODEL), EVEN_OUT ** -0.5),
        'od_w_in': nrm((NO, D_MODEL, ODD_IN), D_MODEL ** -0.5),
        'od_mu': unif((NO, RWKV_IN), 0.0, 1.0),
        'od_w0': unif((NO, RWKV_WIDTH), -6.0, -1.0),
        'od_w2': nrm((NO, DECAY_RANK, RWKV_WIDTH), 0.5 * DECAY_RANK ** -0.5),
        'od_a0': nrm((NO, RWKV_WIDTH), 0.1),
        'od_a2': nrm((NO, ICL_RANK, RWKV_WIDTH), 0.5 * ICL_RANK ** -0.5),
        'od_g2': nrm((NO, GATE_RANK, RWKV_WIDTH), GATE_RANK ** -0.5),
        'od_k_k': 0.85 + 0.1 * jax.random.normal(next(ks), (NO, RWKV_WIDTH), f32),
        'od_k_a': 1.0 + 0.1 * jax.random.normal(next(ks), (NO, RWKV_WIDTH), f32),
        'od_r_k': nrm((NO, RWKV_HEADS, RWKV_HEAD_DIM), 0.1),
        'od_lnx_w': gain((NO, RWKV_WIDTH)),
        'od_lnx_b': nrm((NO, RWKV_WIDTH), 0.02),
        'od_s5_a_re': -0.5 + nrm((NO, S5_GROUPS, S5_STATE), 0.01),
        'od_s5_a_im': math.pi * jnp.arange(S5_STATE, dtype=f32)[None, None, :] + nrm((NO, S5_GROUPS, S5_STATE), 0.01),
        'od_s5_log_step': unif((NO, S5_GROUPS), math.log(1e-3), math.log(1e-1)),
        'od_s5_b_re': nrm((NO, S5_GROUPS, S5_STATE, S5_GROUP), (2 * S5_GROUP) ** -0.5),
        'od_s5_b_im': nrm((NO, S5_GROUPS, S5_STATE, S5_GROUP), (2 * S5_GROUP) ** -0.5),
        'od_s5_c_re': nrm((NO, S5_GROUPS, S5_GROUP, S5_STATE), (2 * S5_STATE) ** -0.5),
        'od_s5_c_im': nrm((NO, S5_GROUPS, S5_GROUP, S5_STATE), (2 * S5_STATE) ** -0.5),
        'od_s5_d': nrm((NO, S5_GROUPS, S5_GROUP), 1.0),
        'od_glu_w': nrm((NO, S5_WIDTH, S5_WIDTH), S5_WIDTH ** -0.5),
        'od_glu_b': nrm((NO, S5_WIDTH), 0.02),
        'od_w_out': nrm((NO, ODD_OUT, D_MODEL), ODD_OUT ** -0.5),
        'moe_wg': nrm((DEPTH, D_MODEL, N_EXPERT_GROUPS), D_MODEL ** -0.5),
        'moe_bg': nrm((DEPTH, N_EXPERT_GROUPS), 0.01),
        'moe_we': nrm((DEPTH, D_MODEL, N_EXPERTS), D_MODEL ** -0.5),
        'moe_be': nrm((DEPTH, N_EXPERTS), 0.01),
        'moe_w1': nrm((DEPTH, N_EXPERTS, D_MODEL, EXPERT_FF), D_MODEL ** -0.5),
        'moe_w3': nrm((DEPTH, N_EXPERTS, D_MODEL, EXPERT_FF), D_MODEL ** -0.5),
        'moe_w2': nrm((DEPTH, N_EXPERTS, EXPERT_FF, D_MODEL), EXPERT_FF ** -0.5),
    }


def reference(x, norm_mix_w, norm_ffn_w, ev_w_in, ev_q_norm_w, ev_k_norm_w, ev_conv_w, ev_conv_b,
              ev_dt_bias, ev_a_log, ev_d_skip, ev_ssd_norm_w, ev_w_out, od_w_in, od_mu, od_w0, od_w2,
              od_a0, od_a2, od_g2, od_k_k, od_k_a, od_r_k, od_lnx_w, od_lnx_b, od_s5_a_re, od_s5_a_im,
              od_s5_log_step, od_s5_b_re, od_s5_b_im, od_s5_c_re, od_s5_c_im, od_s5_d, od_glu_w, od_glu_b,
              od_w_out, moe_wg, moe_bg, moe_we, moe_be, moe_w1, moe_w3, moe_w2):
    pos = jnp.arange(x.shape[1], dtype=jnp.int32)
    h = x
    for layer in range(DEPTH):
        i = layer // 2
        xn = rms_norm(h, norm_mix_w[layer])
        if layer % 2 == 0:
            mix = even_mixer(xn, pos, ev_w_in[i], ev_q_norm_w[i], ev_k_norm_w[i], ev_conv_w[i], ev_conv_b[i],
                             ev_dt_bias[i], ev_a_log[i], ev_d_skip[i], ev_ssd_norm_w[i], ev_w_out[i])
        else:
            mix = odd_mixer(xn, od_w_in[i], od_mu[i], od_w0[i], od_w2[i], od_a0[i], od_a2[i], od_g2[i],
                            od_k_k[i], od_k_a[i], od_r_k[i], od_lnx_w[i], od_lnx_b[i], od_s5_a_re[i],
                            od_s5_a_im[i], od_s5_log_step[i], od_s5_b_re[i], od_s5_b_im[i], od_s5_c_re[i],
                            od_s5_c_im[i], od_s5_d[i], od_glu_w[i], od_glu_b[i], od_w_out[i])
        h = h + mix.astype(h.dtype)
        ffn = hierarchical_moe(rms_norm(h, norm_ffn_w[layer]), moe_wg[layer], moe_bg[layer], moe_we[layer],
                               moe_be[layer], moe_w1[layer], moe_w3[layer], moe_w2[layer])
        h = h + ffn.astype(h.dtype)
    return h
```

```python
import functools
import math

import jax
import jax.numpy as jnp
import numpy as np
from jax import lax
from jax.experimental import pallas as pl
from jax.experimental.pallas import tpu as pltpu

F32 = jnp.float32
BF16 = jnp.bfloat16

LANES = 128
SUBLANES = 8
VMEM_LIMIT_BYTES = 52 * 1024 * 1024

NORM_EPS = 1e-6
NEG_BIG = -1e30

ATT_HEADS = 16
ATT_HEAD_DIM = 128
ATT_WIDTH = ATT_HEADS * ATT_HEAD_DIM
DILATED_BRANCHES = ((128, 1), (512, 4), (2048, 16))
ROPE_THETA = 500000.0
ROPE_DIM = ATT_HEAD_DIM // 4
SSD_HEADS = 32
SSD_HEAD_DIM = 64
SSD_WIDTH = SSD_HEADS * SSD_HEAD_DIM
SSD_GROUPS = 8
SSD_STATE = 128
SSD_CONV = 4
SSD_CHUNK = 128
SSD_CONV_CH = SSD_WIDTH + 2 * SSD_GROUPS * SSD_STATE
RWKV_HEADS = 32
RWKV_HEAD_DIM = 64
RWKV_WIDTH = RWKV_HEADS * RWKV_HEAD_DIM
DECAY_RANK = 96
ICL_RANK = 96
GATE_RANK = 256
RWKV_GN_EPS = 64e-5
RWKV_CHUNK = 64
S5_GROUP = 16
S5_GROUPS = 128
S5_WIDTH = S5_GROUP * S5_GROUPS
S5_STATE = 64
S5_GB = 8
N_EXPERT_GROUPS = 4
EXPERTS_PER_GROUP = 8
N_EXPERTS = N_EXPERT_GROUPS * EXPERTS_PER_GROUP
TOP_K = 2
EXPERT_FF = 1024
MOE_ROWS = 256


def _cparams(*sem):
    return pltpu.CompilerParams(dimension_semantics=sem, vmem_limit_bytes=VMEM_LIMIT_BYTES)


def _split3(x):
    hi = x.astype(BF16)
    r1 = x - hi.astype(F32)
    mid = r1.astype(BF16)
    lo = (r1 - mid.astype(F32)).astype(BF16)
    return hi, mid, lo


def _dot(a, b):
    return jnp.dot(a, b, preferred_element_type=F32)


def _dot_nt(a, b):
    return lax.dot_general(a, b, (((1,), (1,)), ((), ())), preferred_element_type=F32)


def _dot_tn(a, b):
    return lax.dot_general(a, b, (((0,), (0,)), ((), ())), preferred_element_type=F32)


def _dot_exact_lhs(l_bf16, x):
    hi, mid, lo = _split3(x)
    return _dot(l_bf16, hi) + _dot(l_bf16, mid) + _dot(l_bf16, lo)


def _dot_exact_rhs(x, r_bf16):
    hi, mid, lo = _split3(x)
    return _dot(hi, r_bf16) + _dot(mid, r_bf16) + _dot(lo, r_bf16)


def _dot_hi(a, b):
    a1, a2, a3 = _split3(a)
    b1, b2, b3 = _split3(b)
    return (_dot(a1, b1) + (_dot(a1, b2) + _dot(a2, b1))
            + (_dot(a1, b3) + _dot(a2, b2) + _dot(a3, b1)))


def _dot_hi_nt(a, b):
    a1, a2, a3 = _split3(a)
    b1, b2, b3 = _split3(b)
    return (_dot_nt(a1, b1) + (_dot_nt(a1, b2) + _dot_nt(a2, b1))
            + (_dot_nt(a1, b3) + _dot_nt(a2, b2) + _dot_nt(a3, b1)))


def _sigmoid(x):
    return 1.0 / (1.0 + jnp.exp(-x))


def _softplus(x):
    return jnp.maximum(x, 0.0) + jnp.log(1.0 + jnp.exp(-jnp.abs(x)))


def _rmsnorm_kernel(x_ref, w_ref, o_ref):
    x = x_ref[...]
    ms = jnp.mean(x * x, axis=-1, keepdims=True)
    o_ref[...] = (x * lax.rsqrt(ms + NORM_EPS) * w_ref[...]).astype(o_ref.dtype)


def _rmsnorm(x, w, out_dtype, rows=256):
    t, d = x.shape
    rows = min(rows, t)
    return pl.pallas_call(
        _rmsnorm_kernel,
        grid=(t // rows,),
        in_specs=[pl.BlockSpec((rows, d), lambda i: (i, 0)),
                  pl.BlockSpec((1, d), lambda i: (0, 0))],
        out_specs=pl.BlockSpec((rows, d), lambda i: (i, 0)),
        out_shape=jax.ShapeDtypeStruct((t, d), out_dtype),
        compiler_params=_cparams("parallel"),
    )(x, w.reshape(1, d))


def _mm_kernel(*refs, n_in, epilogue):
    a_refs = refs[:n_in]
    w_refs = refs[n_in:2 * n_in]
    rest = refs[2 * n_in:]
    o_ref = rest[-1]
    acc = _dot(a_refs[0][...], w_refs[0][...])
    for a_ref, w_ref in zip(a_refs[1:], w_refs[1:]):
        acc = acc + _dot(a_ref[...], w_ref[...])
    if epilogue == "residual":
        acc = rest[0][...] + acc
    elif epilogue == "glu":
        acc = rest[0][...] * _sigmoid(acc + rest[1][...])
    o_ref[...] = acc.astype(o_ref.dtype)


def _matmul(a_list, w_list, out_dtype, *, tm=512, tn=1024, epilogue="none", extra=()):
    m = a_list[0].shape[0]
    n = w_list[0].shape[1]
    tm = min(tm, m)
    tn = min(tn, n)
    assert m % tm == 0 and n % tn == 0, (m, n, tm, tn)
    in_specs = []
    for a in a_list:
        in_specs.append(pl.BlockSpec((tm, a.shape[1]), lambda j, i: (i, 0)))
    for w in w_list:
        in_specs.append(pl.BlockSpec((w.shape[0], tn), lambda j, i: (0, j)))
    if epilogue == "residual":
        in_specs.append(pl.BlockSpec((tm, tn), lambda j, i: (i, j)))
    elif epilogue == "glu":
        in_specs.append(pl.BlockSpec((tm, tn), lambda j, i: (i, j)))
        in_specs.append(pl.BlockSpec((1, tn), lambda j, i: (0, j)))
    return pl.pallas_call(
        functools.partial(_mm_kernel, n_in=len(a_list), epilogue=epilogue),
        grid=(n // tn, m // tm),
        in_specs=in_specs,
        out_specs=pl.BlockSpec((tm, tn), lambda j, i: (i, j)),
        out_shape=jax.ShapeDtypeStruct((m, n), out_dtype),
        compiler_params=_cparams("parallel", "parallel"),
    )(*a_list, *w_list, *extra)


def _qk_prep_kernel(x_ref, w_ref, cos_ref, sin_ref, o_ref):
    x = x_ref[...]
    ms = jnp.mean(x * x, axis=-1, keepdims=True)
    y = x * lax.rsqrt(ms + NORM_EPS) * w_ref[0]
    half = ROPE_DIM // 2
    lane = lax.broadcasted_iota(jnp.int32, y.shape, 1)
    partner = jnp.where(lane < half, pltpu.roll(y, LANES - half, 1), pltpu.roll(y, half, 1))
    o_ref[...] = (y * cos_ref[...] + partner * sin_ref[...]).astype(o_ref.dtype)


def _qk_prep(proj, qk_w, cos_f, sin_f, rows=512):
    t = proj.shape[0]
    rows = min(rows, t)
    nslot = 2 * ATT_HEADS
    return pl.pallas_call(
        _qk_prep_kernel,
        grid=(t // rows, nslot),
        in_specs=[pl.BlockSpec((rows, LANES), lambda i, s: (i, s)),
                  pl.BlockSpec((1, 1, LANES), lambda i, s: (s // ATT_HEADS, 0, 0)),
                  pl.BlockSpec((rows, LANES), lambda i, s: (i, 0)),
                  pl.BlockSpec((rows, LANES), lambda i, s: (i, 0))],
        out_specs=pl.BlockSpec((rows, LANES), lambda i, s: (i, s)),
        out_shape=jax.ShapeDtypeStruct((t, nslot * LANES), BF16),
        compiler_params=_cparams("parallel", "parallel"),
    )(proj, qk_w, cos_f, sin_f)


def _branch_kernel(q_ref, kc_ref, kp_ref, vc_ref, vp_ref, o_ref, lse_ref):
    n = pl.program_id(2)
    w = q_ref.shape[0]
    scale = ATT_HEAD_DIM ** -0.5
    q = q_ref[...]
    s_c = _dot_nt(q, kc_ref[...]) * scale
    s_p = _dot_nt(q, kp_ref[...]) * scale
    qi = lax.broadcasted_iota(jnp.int32, (w, w), 0)
    kj = lax.broadcasted_iota(jnp.int32, (w, w), 1)
    s_c = jnp.where(kj <= qi, s_c, NEG_BIG)
    s_p = jnp.where(jnp.logical_and(kj >= qi, n > 0), s_p, NEG_BIG)
    m = jnp.maximum(jnp.max(s_c, axis=-1, keepdims=True), jnp.max(s_p, axis=-1, keepdims=True))
    p_c = jnp.exp(s_c - m)
    p_p = jnp.exp(s_p - m)
    l = jnp.sum(p_c, axis=-1, keepdims=True) + jnp.sum(p_p, axis=-1, keepdims=True)
    o = _dot(p_c.astype(BF16), vc_ref[...].astype(BF16)) + _dot(p_p.astype(BF16), vp_ref[...].astype(BF16))
    o_ref[...] = o / l
    lse_ref[...] = jnp.broadcast_to(m + jnp.log(l), o.shape)


def _dilated_branch(qk, proj, window, dilation):
    t = qk.shape[0]
    np_cols = proj.shape[1]
    w = window // dilation
    l = t // dilation
    assert l % w == 0 and w == LANES
    nb = l // w
    qk_v = qk.reshape(l, dilation * 2 * ATT_WIDTH)
    proj_v = proj.reshape(l, dilation * np_cols)
    qk_blk = 2 * ATT_HEADS
    p_blk = np_cols // LANES
    prev = lambda n: jnp.maximum(n - 1, 0)
    o, lse = pl.pallas_call(
        _branch_kernel,
        grid=(dilation, ATT_HEADS, nb),
        in_specs=[
            pl.BlockSpec((w, LANES), lambda j, h, n: (n, j * qk_blk + h)),
            pl.BlockSpec((w, LANES), lambda j, h, n: (n, j * qk_blk + ATT_HEADS + h)),
            pl.BlockSpec((w, LANES), lambda j, h, n: (prev(n), j * qk_blk + ATT_HEADS + h)),
            pl.BlockSpec((w, LANES), lambda j, h, n: (n, j * p_blk + 2 * ATT_HEADS + h)),
            pl.BlockSpec((w, LANES), lambda j, h, n: (prev(n), j * p_blk + 2 * ATT_HEADS + h)),
        ],
        out_specs=[pl.BlockSpec((w, LANES), lambda j, h, n: (n, j * ATT_HEADS + h)),
                   pl.BlockSpec((w, LANES), lambda j, h, n: (n, j * ATT_HEADS + h))],
        out_shape=[jax.ShapeDtypeStruct((l, dilation * ATT_WIDTH), F32),
                   jax.ShapeDtypeStruct((l, dilation * ATT_WIDTH), F32)],
        compiler_params=_cparams("parallel", "parallel", "arbitrary"),
    )(qk_v, qk_v, qk_v, proj_v, proj_v)
    return o.reshape(t, ATT_WIDTH), lse.reshape(t, ATT_WIDTH)


def _merge_kernel(o1, o2, o3, l1, l2, l3, out_ref):
    a, b, c = l1[...], l2[...], l3[...]
    m = jnp.maximum(jnp.maximum(a, b), c)
    wa, wb, wc = jnp.exp(a - m), jnp.exp(b - m), jnp.exp(c - m)
    out_ref[...] = ((wa * o1[...] + wb * o2[...] + wc * o3[...]) / (wa + wb + wc)).astype(out_ref.dtype)


def _merge_branches(outs, lses, rows=256):
    t, wdt = outs[0].shape
    rows = min(rows, t)
    spec = pl.BlockSpec((rows, wdt), lambda i: (i, 0))
    return pl.pallas_call(
        _merge_kernel,
        grid=(t // rows,),
        in_specs=[spec] * 6,
        out_specs=spec,
        out_shape=jax.ShapeDtypeStruct((t, wdt), BF16),
        compiler_params=_cparams("parallel"),
    )(*outs, *lses)


def _rope_tables(t):
    half = ROPE_DIM // 2
    inv_freq = ROPE_THETA ** (-jnp.arange(half, dtype=F32) / half)
    ang = jnp.arange(t, dtype=jnp.int32).astype(F32)[:, None] * inv_freq[None, :]
    cos, sin = jnp.cos(ang), jnp.sin(ang)
    ones = jnp.ones((t, ATT_HEAD_DIM - ROPE_DIM), F32)
    cos_f = jnp.concatenate([cos, cos, ones], axis=1)
    sin_f = jnp.concatenate([-sin, sin, 0.0 * ones], axis=1)
    return cos_f, sin_f


def _dilated_attention(proj, q_norm_w, k_norm_w):
    t = proj.shape[0]
    cos_f, sin_f = _rope_tables(t)
    qk_w = jnp.stack([q_norm_w, k_norm_w]).reshape(2, 1, ATT_HEAD_DIM)
    qk = _qk_prep(proj, qk_w, cos_f, sin_f)
    outs, lses = [], []
    for window, dilation in DILATED_BRANCHES:
        o, lse = _dilated_branch(qk, proj, window, dilation)
        outs.append(o)
        lses.append(lse)
    return _merge_branches(outs, lses)


def _ssd_kernel(z_ref, xbc_ref, xbcp_ref, dt_ref, dtt_ref, convw_ref, convb_ref, dtb_ref, dtbt_ref,
                alog_ref, alogt_ref, dskip_ref, normw_ref, o_ref, state_ref, y_ref):
    c = pl.program_id(0)
    q = SSD_CHUNK

    @pl.when(c == 0)
    def _():
        state_ref[...] = jnp.zeros_like(state_ref)

    cur = xbc_ref[...]
    prev = xbcp_ref[...] * jnp.where(c > 0, 1.0, 0.0)
    row = lax.broadcasted_iota(jnp.int32, cur.shape, 0)
    acc = convb_ref[...] + convw_ref[SSD_CONV - 1:SSD_CONV, :] * cur
    for k in range(1, SSD_CONV):
        sh = jnp.where(row < k, pltpu.roll(prev, k, 0), pltpu.roll(cur, k, 0))
        acc = acc + convw_ref[SSD_CONV - 1 - k:SSD_CONV - k, :] * sh
    xbc = acc * _sigmoid(acc)
    xs = xbc[:, :SSD_WIDTH]
    gn = SSD_GROUPS * SSD_STATE
    bm = xbc[:, SSD_WIDTH:SSD_WIDTH + gn].astype(BF16)
    cm = xbc[:, SSD_WIDTH + gn:].astype(BF16)

    li = lax.broadcasted_iota(jnp.int32, (q, q), 0)
    si = lax.broadcasted_iota(jnp.int32, (q, q), 1)
    causal = li >= si
    tri = jnp.where(causal, 1.0, 0.0).astype(BF16)
    tri_t = jnp.where(li <= si, 1.0, 0.0).astype(BF16)

    dt = _softplus(dt_ref[...] + dtb_ref[...])
    acs = _dot_exact_lhs(tri, dt * (-jnp.exp(alog_ref[...])))
    dt_t = _softplus(dtt_ref[...] + dtbt_ref[...])
    acs_t = _dot_exact_rhs(dt_t * (-jnp.exp(alogt_ref[...])), tri_t)

    epg = SSD_HEADS // SSD_GROUPS
    for g in range(SSD_GROUPS):
        bg = bm[:, g * SSD_STATE:(g + 1) * SSD_STATE]
        cg = cm[:, g * SSD_STATE:(g + 1) * SSD_STATE]
        cb = _dot_nt(cg, bg)
        for e in range(epg):
            h = g * epg + e
            col = acs[:, h:h + 1]
            rowv = acs_t[h:h + 1, :]
            last = acs[q - 1:q, h:h + 1]
            decay = jnp.exp(jnp.where(causal, col - rowv, NEG_BIG))
            xh = xs[:, h * SSD_HEAD_DIM:(h + 1) * SSD_HEAD_DIM]
            xdt = xh * dt[:, h:h + 1]
            hstate = state_ref[h]
            y = _dot((cb * decay).astype(BF16), xdt.astype(BF16))
            y = y + _dot_nt(cg, hstate.astype(BF16)) * jnp.exp(col)
            st = _dot_tn((xdt * jnp.exp(last - col)).astype(BF16), bg)
            state_ref[h] = hstate * jnp.exp(last) + st
            y_ref[:, h * SSD_HEAD_DIM:(h + 1) * SSD_HEAD_DIM] = y
    z = z_ref[...]
    y = (y_ref[...] + dskip_ref[...] * xs) * (z * _sigmoid(z))
    ms = jnp.mean(y * y, axis=-1, keepdims=True)
    o_ref[...] = (y * lax.rsqrt(ms + NORM_EPS) * normw_ref[...]).astype(o_ref.dtype)


def _mamba2(proj, dt_pad, conv_w, conv_b, dt_bias, a_log, d_skip, norm_w):
    t = proj.shape[0]
    q = SSD_CHUNK
    nc = t // q
    pad = LANES - SSD_HEADS
    dtb = jnp.pad(dt_bias, (0, pad)).reshape(1, LANES)
    alog = jnp.pad(a_log, (0, pad)).reshape(1, LANES)
    dskip = jnp.repeat(d_skip, SSD_HEAD_DIM).reshape(1, SSD_WIDTH)
    z_blk = (3 * ATT_WIDTH) // SSD_WIDTH
    xbc_blk = (3 * ATT_WIDTH + SSD_WIDTH) // SSD_CONV_CH
    assert z_blk * SSD_WIDTH == 3 * ATT_WIDTH and xbc_blk * SSD_CONV_CH == 3 * ATT_WIDTH + SSD_WIDTH
    const = lambda shape: pl.BlockSpec(shape, lambda c: (0,) * len(shape))
    return pl.pallas_call(
        _ssd_kernel,
        grid=(nc,),
        in_specs=[
            pl.BlockSpec((q, SSD_WIDTH), lambda c: (c, z_blk)),
            pl.BlockSpec((q, SSD_CONV_CH), lambda c: (c, xbc_blk)),
            pl.BlockSpec((q, SSD_CONV_CH), lambda c: (jnp.maximum(c - 1, 0), xbc_blk)),
            pl.BlockSpec((q, LANES), lambda c: (c, 0)),
            pl.BlockSpec((LANES, q), lambda c: (0, c)),
            const((SSD_CONV, SSD_CONV_CH)), const((1, SSD_CONV_CH)),
            const((1, LANES)), const((LANES, 1)), const((1, LANES)), const((LANES, 1)),
            const((1, SSD_WIDTH)), const((1, SSD_WIDTH)),
        ],
        out_specs=pl.BlockSpec((q, SSD_WIDTH), lambda c: (c, 0)),
        out_shape=jax.ShapeDtypeStruct((t, SSD_WIDTH), BF16),
        scratch_shapes=[pltpu.VMEM((SSD_HEADS, SSD_HEAD_DIM, SSD_STATE), F32),
                        pltpu.VMEM((q, SSD_WIDTH), F32)],
        compiler_params=_cparams("arbitrary"),
    )(proj, proj, proj, dt_pad, dt_pad.T, conv_w, conv_b.reshape(1, -1), dtb, dtb.reshape(LANES, 1),
      alog, alog.reshape(LANES, 1), dskip, norm_w.reshape(1, -1))


RWKV_PC = 3 * RWKV_WIDTH + LANES + LANES + GATE_RANK


def _rwkv_prep_kernel(pc_ref, pp_ref, mu_ref, w0_ref, w2_ref, a0_ref, a2_ref, g2_ref, kk_ref, ka_ref,
                      hsum_ref, hexp_ref, r_o, k_o, v_o, lw_o, kk_o, a_o, g_o):
    i = pl.program_id(0)
    cur = pc_ref[...]
    row = lax.broadcasted_iota(jnp.int32, cur.shape, 0)
    last_prev = pp_ref[SUBLANES - 1:SUBLANES, :] * jnp.where(i > 0, 1.0, 0.0)
    prev = jnp.where(row == 0, last_prev, pltpu.roll(cur, 1, 0))
    p = cur + (prev - cur) * mu_ref[...]
    w = RWKV_WIDTH
    r, k, v = p[:, :w], p[:, w:2 * w], p[:, 2 * w:3 * w]
    wd = p[:, 3 * w:3 * w + LANES]
    ad = p[:, 3 * w + LANES:3 * w + 2 * LANES]
    gd = p[:, 3 * w + 2 * LANES:]
    w_log = -_softplus(-(w0_ref[...] + _dot(jnp.tanh(wd).astype(BF16), w2_ref[...]))) - 0.5
    a = _sigmoid(a0_ref[...] + _dot(ad.astype(BF16), a2_ref[...]))
    g = _dot(_sigmoid(gd).astype(BF16), g2_ref[...])
    kkr = k * kk_ref[...]
    ss = _dot_exact_rhs(kkr * kkr, hsum_ref[...])
    nrm = jnp.maximum(jnp.sqrt(ss), 1e-12)
    inv = _dot_exact_rhs(1.0 / nrm, hexp_ref[...])
    r_o[...] = r
    k_o[...] = k * (1.0 + (a - 1.0) * ka_ref[...])
    v_o[...] = v
    lw_o[...] = -jnp.exp(w_log)
    kk_o[...] = kkr * inv
    a_o[...] = a
    g_o[...] = g


def _rwkv_prep(proj, mu, w0, w2, a0, a2, g2, k_k, k_a, rows=256):
    t = proj.shape[0]
    rows = min(rows, t)
    w = RWKV_WIDTH
    head_of = jnp.arange(w) // RWKV_HEAD_DIM
    hsum = (head_of[:, None] == jnp.arange(LANES)[None, :]).astype(BF16)
    hexp = hsum.T
    const = lambda shape: pl.BlockSpec(shape, lambda i: (0,) * len(shape))
    rb = rows // SUBLANES
    outs = pl.pallas_call(
        _rwkv_prep_kernel,
        grid=(t // rows,),
        in_specs=[pl.BlockSpec((rows, RWKV_PC), lambda i: (i, 0)),
                  pl.BlockSpec((SUBLANES, RWKV_PC), lambda i: (jnp.maximum(i * rb - 1, 0), 0)),
                  const((1, RWKV_PC)), const((1, w)), const((LANES, w)), const((1, w)),
                  const((LANES, w)), const((GATE_RANK, w)), const((1, w)), const((1, w)),
                  const((w, LANES)), const((LANES, w))],
        out_specs=[pl.BlockSpec((rows, w), lambda i: (i, 0))] * 7,
        out_shape=[jax.ShapeDtypeStruct((t, w), F32)] * 7,
        compiler_params=_cparams("parallel"),
    )(proj, proj, mu, w0.reshape(1, w), w2, a0.reshape(1, w), a2, g2, k_k.reshape(1, w),
      k_a.reshape(1, w), hsum, hexp)
    return outs


def _rwkv_scan_kernel(r_ref, k_ref, v_ref, lw_ref, kk_ref, a_ref, g_ref, rk_ref, lnw_ref, lnb_ref,
                      o_ref, state_ref, *, heads_per_step):
    ci = pl.program_id(1)
    c = RWKV_CHUNK
    n = RWKV_HEAD_DIM

    @pl.when(ci == 0)
    def _():
        state_ref[...] = jnp.zeros_like(state_ref)

    ti = lax.broadcasted_iota(jnp.int32, (c, c), 0)
    tj = lax.broadcasted_iota(jnp.int32, (c, c), 1)
    tri = jnp.where(ti >= tj, 1.0, 0.0).astype(BF16)
    strict = ti > tj
    incl = ti >= tj
    eye = jnp.where(ti == tj, 1.0, 0.0)

    for hh in range(heads_per_step):
        sl = slice(hh * n, (hh + 1) * n)
        r, k, v = r_ref[:, sl], k_ref[:, sl], v_ref[:, sl]
        lw, kk, a = lw_ref[:, sl], kk_ref[:, sl], a_ref[:, sl]
        s0 = state_ref[hh]
        cum = _dot_exact_lhs(tri, lw)
        w_incl = jnp.exp(cum)
        w_inv = jnp.exp(-cum)
        w_prev = jnp.exp(cum - lw)
        at = -kk * w_prev
        rt = r * w_incl
        bt = kk * a * w_inv
        kt = k * w_inv
        a_ab = jnp.where(strict, _dot_hi_nt(at, bt), 0.0)
        a_ak = jnp.where(strict, _dot_hi_nt(at, kt), 0.0)
        a_rb = jnp.where(incl, _dot_nt(rt.astype(BF16), bt.astype(BF16)), 0.0)
        a_rk = jnp.where(incl, _dot_nt(rt.astype(BF16), kt.astype(BF16)), 0.0)
        tinv = eye + a_ab
        pw = a_ab
        for _ in range(int(math.log2(c)) - 1):
            pw = _dot_hi(pw, pw)
            tinv = tinv + _dot_hi(tinv, pw)
        rhs = _dot_hi_nt(at, s0) + _dot_hi(a_ak, v)
        u = _dot_hi(tinv, rhs)
        ub, vb = u.astype(BF16), v.astype(BF16)
        y = (_dot_nt(rt.astype(BF16), s0.astype(BF16)) + _dot(a_rb.astype(BF16), ub)
             + _dot(a_rk.astype(BF16), vb))
        s_new = (s0 + _dot_tn(ub, bt.astype(BF16)) + _dot_tn(vb, kt.astype(BF16))) * w_incl[c - 1:c, :]
        state_ref[hh] = s_new
        mean = jnp.mean(y, axis=-1, keepdims=True)
        var = jnp.mean(jnp.square(y - mean), axis=-1, keepdims=True)
        yn = (y - mean) * lax.rsqrt(var + RWKV_GN_EPS) * lnw_ref[:, sl] + lnb_ref[:, sl]
        bonus = jnp.sum(r * k * rk_ref[:, sl], axis=-1, keepdims=True) * v
        o_ref[:, sl] = ((yn + bonus) * g_ref[:, sl]).astype(o_ref.dtype)


def _rwkv_scan(r, k, v, lw, kk, a, g, r_k, lnx_w, lnx_b, heads_per_step=2):
    t, w = r.shape
    c = RWKV_CHUNK
    hb = heads_per_step
    wb = hb * RWKV_HEAD_DIM
    seq = pl.BlockSpec((c, wb), lambda h, i: (i, h))
    par = pl.BlockSpec((1, wb), lambda h, i: (0, h))
    return pl.pallas_call(
        functools.partial(_rwkv_scan_kernel, heads_per_step=hb),
        grid=(RWKV_HEADS // hb, t // c),
        in_specs=[seq] * 7 + [par] * 3,
        out_specs=seq,
        out_shape=jax.ShapeDtypeStruct((t, w), BF16),
        scratch_shapes=[pltpu.VMEM((hb, RWKV_HEAD_DIM, RWKV_HEAD_DIM), F32)],
        compiler_params=_cparams("parallel", "arbitrary"),
    )(r, k, v, lw, kk, a, g, r_k.reshape(1, w), lnx_w.reshape(1, w), lnx_b.reshape(1, w))


S5_CH = S5_GB * S5_STATE


def _s5_kernel(u_ref, bb_ref, cd_ref, tab_ref, d_ref, o_ref, xs_ref, carry_ref):
    tt = pl.program_id(1)
    rows = u_ref.shape[0]
    nch = S5_CH

    @pl.when(tt == 0)
    def _():
        carry_ref[...] = jnp.zeros_like(carry_ref)

    u = u_ref[...]
    xs_ref[...] = _dot(u.astype(BF16), bb_ref[0])
    tab = tab_ref[0]
    steps = ((1, tab[0], tab[1]), (2, tab[2], tab[3]), (4, tab[4], tab[5]))
    pr, pi = tab[6], tab[7]

    def body(i, carry):
        cr, ci = carry
        base = pl.multiple_of(i * SUBLANES, SUBLANES)
        xr = xs_ref[pl.ds(base, SUBLANES), :nch]
        xi = xs_ref[pl.ds(base, SUBLANES), nch:]
        for s, mr, mi in steps:
            rr = pltpu.roll(xr, s, 0)
            ri = pltpu.roll(xi, s, 0)
            xr, xi = xr + mr * rr - mi * ri, xi + mr * ri + mi * rr
        xr, xi = xr + pr * cr - pi * ci, xi + pr * ci + pi * cr
        xs_ref[pl.ds(base, SUBLANES), :nch] = xr
        xs_ref[pl.ds(base, SUBLANES), nch:] = xi
        return xr[SUBLANES - 1:SUBLANES, :], xi[SUBLANES - 1:SUBLANES, :]

    cr, ci = lax.fori_loop(0, rows // SUBLANES, body, (carry_ref[0:1, :], carry_ref[1:2, :]))
    carry_ref[0:1, :] = cr
    carry_ref[1:2, :] = ci
    y = _dot(xs_ref[...].astype(BF16), cd_ref[0]) + d_ref[...] * u
    o_ref[...] = 0.5 * y * (1.0 + jnp.tanh(math.sqrt(2.0 / math.pi) * (y + 0.044715 * (y * y * y))))


def _s5_params(a_re, a_im, log_step, b_re, b_im, c_re, c_im):
    lam_re = jnp.minimum(a_re, -1e-4)
    lam_im = a_im
    step = jnp.exp(log_step)[:, None]
    mag = jnp.exp(lam_re * step)
    lb_re = mag * jnp.cos(lam_im * step)
    lb_im = mag * jnp.sin(lam_im * step)
    den = lam_re * lam_re + lam_im * lam_im
    f_re = ((lb_re - 1.0) * lam_re + lb_im * lam_im) / den
    f_im = (lb_im * lam_re - (lb_re - 1.0) * lam_im) / den
    bb_re = f_re[..., None] * b_re - f_im[..., None] * b_im
    bb_im = f_re[..., None] * b_im + f_im[..., None] * b_re
    nb = S5_GROUPS // S5_GB
    eye = jnp.eye(S5_GB, dtype=F32)

    def bdiag_in(m):
        mt = m.transpose(0, 2, 1).reshape(nb, S5_GB, S5_GROUP, S5_STATE)
        return jnp.einsum('bghp,gk->bghkp', mt, eye).reshape(nb, S5_GB * S5_GROUP, S5_CH)

    def bdiag_out(m):
        mt = m.transpose(0, 2, 1).reshape(nb, S5_GB, S5_STATE, S5_GROUP)
        return jnp.einsum('bgph,gk->bgpkh', mt, eye).reshape(nb, S5_CH, S5_GB * S5_GROUP)

    bb = jnp.concatenate([bdiag_in(bb_re), bdiag_in(bb_im)], axis=2).astype(BF16)
    cd = jnp.concatenate([bdiag_out(c_re), -bdiag_out(c_im)], axis=1).astype(BF16)
    pr, pi = [lb_re], [lb_im]
    for _ in range(SUBLANES - 1):
        pr, pi = pr + [pr[-1] * lb_re - pi[-1] * lb_im], pi + [pr[-1] * lb_im + pi[-1] * lb_re]
    pw_r = jnp.stack(pr, axis=0).reshape(SUBLANES, nb, S5_CH).transpose(1, 0, 2)
    pw_i = jnp.stack(pi, axis=0).reshape(SUBLANES, nb, S5_CH).transpose(1, 0, 2)
    rowi = jnp.arange(SUBLANES)[None, :, None]
    kinds = []
    for s in (1, 2, 4):
        kinds.append(jnp.where(rowi >= s, pw_r[:, s - 1:s, :], 0.0))
        kinds.append(jnp.where(rowi >= s, pw_i[:, s - 1:s, :], 0.0))
    kinds += [pw_r, pw_i]
    tab = jnp.stack(kinds, axis=1)
    return bb, cd, tab


def _s5(proj, u_blk0, a_re, a_im, log_step, b_re, b_im, c_re, c_im, d_skip, rows=512):
    t = proj.shape[0]
    rows = min(rows, t)
    nb = S5_GROUPS // S5_GB
    bb, cd, tab = _s5_params(a_re, a_im, log_step, b_re, b_im, c_re, c_im)
    return pl.pallas_call(
        _s5_kernel,
        grid=(nb, t // rows),
        in_specs=[pl.BlockSpec((rows, LANES), lambda b, i: (i, u_blk0 + b)),
                  pl.BlockSpec((1, LANES, 2 * S5_CH), lambda b, i: (b, 0, 0)),
                  pl.BlockSpec((1, 2 * S5_CH, LANES), lambda b, i: (b, 0, 0)),
                  pl.BlockSpec((1, 8, SUBLANES, S5_CH), lambda b, i: (b, 0, 0, 0)),
                  pl.BlockSpec((1, LANES), lambda b, i: (0, b))],
        out_specs=pl.BlockSpec((rows, LANES), lambda b, i: (i, b)),
        out_shape=jax.ShapeDtypeStruct((t, S5_WIDTH), F32),
        scratch_shapes=[pltpu.VMEM((rows, 2 * S5_CH), F32), pltpu.VMEM((SUBLANES, S5_CH), F32)],
        compiler_params=_cparams("parallel", "arbitrary"),
    )(proj, bb, cd, tab, d_skip.reshape(1, S5_WIDTH))


def _router_kernel(h_ref, nw_ref, wr_ref, br_ref, x_ref, idx_ref, wt_ref):
    h = h_ref[...]
    ms = jnp.mean(h * h, axis=-1, keepdims=True)
    x = h * lax.rsqrt(ms + NORM_EPS) * nw_ref[...]
    x_ref[...] = x
    xh, xm, _ = _split3(x)
    logits = (_dot(xh, wr_ref[0]) + (_dot(xm, wr_ref[0]) + _dot(xh, wr_ref[1]))) + br_ref[...]
    lane = lax.broadcasted_iota(jnp.int32, logits.shape, 1)
    big = jnp.int32(4 * LANES)
    ng, epg = N_EXPERT_GROUPS, EXPERTS_PER_GROUP
    lg = jnp.where(lane < ng, logits, NEG_BIG)
    mg = jnp.max(lg, axis=-1, keepdims=True)
    g_sel = jnp.min(jnp.where(lg == mg, lane, big), axis=-1, keepdims=True)
    pg_top = 1.0 / jnp.sum(jnp.exp(lg - mg), axis=-1, keepdims=True)
    lo = ng + g_sel * epg
    le = jnp.where(jnp.logical_and(lane >= lo, lane < lo + epg), logits, NEG_BIG)
    m1 = jnp.max(le, axis=-1, keepdims=True)
    i1 = jnp.min(jnp.where(le == m1, lane, big), axis=-1, keepdims=True)
    zsum = jnp.sum(jnp.exp(le - m1), axis=-1, keepdims=True)
    le2 = jnp.where(lane == i1, NEG_BIG, le)
    m2 = jnp.max(le2, axis=-1, keepdims=True)
    i2 = jnp.min(jnp.where(le2 == m2, lane, big), axis=-1, keepdims=True)
    p1 = 1.0 / zsum
    p2 = jnp.exp(m2 - m1) / zsum
    c1 = pg_top * p1 / (p1 + p2)
    c2 = pg_top * p2 / (p1 + p2)
    idx_ref[...] = jnp.where(lane == 0, i1 - ng, jnp.where(lane == 1, i2 - ng, 0))
    wt_ref[...] = jnp.where(lane == 0, c1, jnp.where(lane == 1, c2, 0.0))


def _router(h, norm_w, wg, bg, we, be, rows=256):
    t, d = h.shape
    rows = min(rows, t)
    ncol = N_EXPERT_GROUPS + N_EXPERTS
    wr = jnp.pad(jnp.concatenate([wg, we], axis=1), ((0, 0), (0, LANES - ncol)))
    wr_hi = wr.astype(BF16)
    wr_mid = (wr - wr_hi.astype(F32)).astype(BF16)
    wr2 = jnp.stack([wr_hi, wr_mid])
    br = jnp.pad(jnp.concatenate([bg, be]), (0, LANES - ncol)).reshape(1, LANES)
    return pl.pallas_call(
        _router_kernel,
        grid=(t // rows,),
        in_specs=[pl.BlockSpec((rows, d), lambda i: (i, 0)),
                  pl.BlockSpec((1, d), lambda i: (0, 0)),
                  pl.BlockSpec((2, d, LANES), lambda i: (0, 0, 0)),
                  pl.BlockSpec((1, LANES), lambda i: (0, 0))],
        out_specs=[pl.BlockSpec((rows, d), lambda i: (i, 0)),
                   pl.BlockSpec((rows, LANES), lambda i: (i, 0)),
                   pl.BlockSpec((rows, LANES), lambda i: (i, 0))],
        out_shape=[jax.ShapeDtypeStruct((t, d), F32),
                   jax.ShapeDtypeStruct((t, LANES), jnp.int32),
                   jax.ShapeDtypeStruct((t, LANES), F32)],
        compiler_params=_cparams("parallel"),
    )(h, norm_w.reshape(1, d), wr2, br)


def _row_copy(src_hbm, dst_ref, sem, src_row, dst_row):
    return pltpu.make_async_copy(src_hbm.at[pl.ds(src_row, 1)], dst_ref.at[pl.ds(dst_row, 1)], sem)


def _gather_rows_kernel(tok_ref, x_hbm, o_hbm, sem, *, rows):
    base = pl.program_id(0) * rows

    def start(i, _):
        _row_copy(x_hbm, o_hbm, sem, tok_ref[base + i], base + i).start()
        return 0

    def wait(i, _):
        _row_copy(x_hbm, o_hbm, sem, tok_ref[base + i], base + i).wait()
        return 0

    lax.fori_loop(0, rows, start, 0)
    lax.fori_loop(0, rows, wait, 0)


def _gather_rows(x, tok, rows=256):
    n = tok.shape[0]
    d = x.shape[1]
    return pl.pallas_call(
        functools.partial(_gather_rows_kernel, rows=rows),
        grid_spec=pltpu.PrefetchScalarGridSpec(
            num_scalar_prefetch=1,
            grid=(n // rows,),
            in_specs=[pl.BlockSpec(memory_space=pl.ANY)],
            out_specs=pl.BlockSpec(memory_space=pl.ANY),
            scratch_shapes=[pltpu.SemaphoreType.DMA],
        ),
        out_shape=jax.ShapeDtypeStruct((n, d), x.dtype),
        compiler_params=_cparams("arbitrary"),
    )(tok, x)


def _expert_kernel(be_ref, nv_ref, x_ref, w1_ref, w3_ref, w2_ref, o_ref):
    b = pl.program_id(0)
    f = pl.program_id(1)

    @pl.when(f == 0)
    def _():
        o_ref[...] = jnp.zeros_like(o_ref)

    @pl.when(nv_ref[b] > 0)
    def _():
        x = x_ref[...].astype(BF16)
        h1 = _dot(x, w1_ref[0])
        h3 = _dot(x, w3_ref[0])
        hb = (h1 * _sigmoid(h1) * h3).astype(BF16)
        o_ref[...] += _dot(hb, w2_ref[0])


def _experts(xg, block_e, nvalid, w1, w3, w2, ff_tile=512):
    n, d = xg.shape
    rows = MOE_ROWS
    ff = w1.shape[2]
    ff_tile = min(ff_tile, ff)
    nf = ff // ff_tile
    ftile = lambda b, f, be, nv: jnp.where(nv[b] > 0, f, nf - 1)
    return pl.pallas_call(
        _expert_kernel,
        grid_spec=pltpu.PrefetchScalarGridSpec(
            num_scalar_prefetch=2,
            grid=(n // rows, nf),
            in_specs=[pl.BlockSpec((rows, d), lambda b, f, be, nv: (b, 0)),
                      pl.BlockSpec((1, d, ff_tile), lambda b, f, be, nv: (be[b], 0, ftile(b, f, be, nv))),
                      pl.BlockSpec((1, d, ff_tile), lambda b, f, be, nv: (be[b], 0, ftile(b, f, be, nv))),
                      pl.BlockSpec((1, ff_tile, d), lambda b, f, be, nv: (be[b], ftile(b, f, be, nv), 0))],
            out_specs=pl.BlockSpec((rows, d), lambda b, f, be, nv: (b, 0)),
        ),
        out_shape=jax.ShapeDtypeStruct((n, d), F32),
        compiler_params=_cparams("arbitrary", "arbitrary"),
    )(block_e, nvalid, xg, w1, w3, w2)


def _combine_kernel(dest_ref, h_ref, wt_ref, yb_hbm, o_ref, buf_ref, sem, *, rows):
    base = pl.program_id(0) * rows

    def start(i, _):
        for j in range(TOP_K):
            _row_copy(yb_hbm, buf_ref.at[j], sem, dest_ref[(base + i) * TOP_K + j], i).start()
        return 0

    def wait(i, _):
        for j in range(TOP_K):
            _row_copy(yb_hbm, buf_ref.at[j], sem, dest_ref[(base + i) * TOP_K + j], i).wait()
        return 0

    lax.fori_loop(0, rows, start, 0)
    lax.fori_loop(0, rows, wait, 0)
    wt = wt_ref[...]
    o_ref[...] = h_ref[...] + wt[:, 0:1] * buf_ref[0] + wt[:, 1:2] * buf_ref[1]


def _combine(h, wt, yb, dest, rows=128):
    t, d = h.shape
    rows = min(rows, t)
    return pl.pallas_call(
        functools.partial(_combine_kernel, rows=rows),
        grid_spec=pltpu.PrefetchScalarGridSpec(
            num_scalar_prefetch=1,
            grid=(t // rows,),
            in_specs=[pl.BlockSpec((rows, d), lambda i, dr: (i, 0)),
                      pl.BlockSpec((rows, LANES), lambda i, dr: (i, 0)),
                      pl.BlockSpec(memory_space=pl.ANY)],
            out_specs=pl.BlockSpec((rows, d), lambda i, dr: (i, 0)),
            scratch_shapes=[pltpu.VMEM((TOP_K, rows, d), F32), pltpu.SemaphoreType.DMA],
        ),
        out_shape=jax.ShapeDtypeStruct((t, d), F32),
        compiler_params=_cparams("arbitrary"),
    )(dest, h, wt, yb)


def _routing_tables(expert_idx, rows):
    t = expert_idx.shape[0]
    a = t * TOP_K
    e_n = N_EXPERTS
    flat_e = expert_idx.reshape(-1)
    onehot = (flat_e[:, None] == jnp.arange(e_n, dtype=jnp.int32)[None, :]).astype(jnp.int32)
    counts = jnp.sum(onehot, axis=0)
    rank = jnp.take_along_axis(jnp.cumsum(onehot, axis=0), flat_e[:, None], axis=1)[:, 0] - 1
    padded = (counts + rows - 1) // rows * rows
    pad_end = jnp.cumsum(padded)
    pad_start = pad_end - padded
    start = jnp.cumsum(counts) - counts
    dest = (pad_start[flat_e] + rank).astype(jnp.int32)
    n_blocks = a // rows + e_n
    order = jnp.argsort(flat_e)
    blk_start = jnp.arange(n_blocks, dtype=jnp.int32) * rows
    block_e = jnp.minimum(jnp.searchsorted(pad_end, blk_start, side='right'), e_n - 1).astype(jnp.int32)
    nvalid = jnp.clip(counts[block_e] - (blk_start - pad_start[block_e]), 0, rows).astype(jnp.int32)
    slots = jnp.arange(n_blocks * rows, dtype=jnp.int32)
    e_s = block_e[slots // rows]
    off = slots - pad_start[e_s]
    valid = jnp.logical_and(off >= 0, off < counts[e_s])
    src = jnp.clip(start[e_s] + off, 0, a - 1)
    slot_tok = jnp.where(valid, order[src] // TOP_K, 0).astype(jnp.int32)
    return slot_tok, dest, block_e, nvalid


def _moe(h, norm_w, wg, bg, we, be, w1, w3, w2):
    x, idx, wt = _router(h, norm_w, wg, bg, we, be)
    slot_tok, dest, block_e, nvalid = _routing_tables(idx[:, :TOP_K], MOE_ROWS)
    xg = _gather_rows(x, slot_tok, rows=MOE_ROWS)
    yb = _experts(xg, block_e, nvalid, w1, w3, w2)
    return _combine(h, wt, yb, dest)


def _even_layer(h, norm_w, w_in, q_norm_w, k_norm_w, conv_w, conv_b, dt_bias, a_log, d_skip, ssd_norm_w, w_out):
    xn = _rmsnorm(h, norm_w, BF16)
    n_main = 3 * ATT_WIDTH + SSD_WIDTH + SSD_CONV_CH
    proj = _matmul([xn], [w_in[:, :n_main].astype(BF16)], F32)
    w_dt = jnp.pad(w_in[:, n_main:], ((0, 0), (0, LANES - SSD_HEADS))).astype(BF16)
    dt_pad = _matmul([xn], [w_dt], F32)
    y_att = _dilated_attention(proj, q_norm_w, k_norm_w)
    y_ssd = _mamba2(proj, dt_pad, conv_w, conv_b, dt_bias, a_log, d_skip, ssd_norm_w)
    wo = w_out.astype(BF16)
    return _matmul([y_att, y_ssd], [wo[:ATT_WIDTH], wo[ATT_WIDTH:]], F32, epilogue="residual", extra=(h,))


def _pad_cols(w, to):
    return jnp.pad(w, ((0, 0), (0, to - w.shape[1])))


def _odd_layer(h, norm_w, w_in, mu, w0, w2, a0, a2, g2, k_k, k_a, r_k, lnx_w, lnx_b,
               s5_a_re, s5_a_im, s5_log_step, s5_b_re, s5_b_im, s5_c_re, s5_c_im, s5_d,
               glu_w, glu_b, w_out):
    xn = _rmsnorm(h, norm_w, BF16)
    w3 = 3 * RWKV_WIDTH
    c_wd, c_ad, c_gd, c_u = w3, w3 + DECAY_RANK, w3 + DECAY_RANK + ICL_RANK, w3 + DECAY_RANK + ICL_RANK + GATE_RANK
    segs = lambda m: jnp.concatenate(
        [m[:, :c_wd], _pad_cols(m[:, c_wd:c_ad], LANES), _pad_cols(m[:, c_ad:c_gd], LANES), m[:, c_gd:]], axis=1)
    w_in_p = segs(w_in).astype(BF16)
    mu_p = segs(mu.reshape(1, -1))[:, :RWKV_PC]
    proj = _matmul([xn], [w_in_p], F32, tn=512)
    w2_p = jnp.pad(w2, ((0, LANES - DECAY_RANK), (0, 0))).astype(BF16)
    a2_p = jnp.pad(a2, ((0, LANES - ICL_RANK), (0, 0))).astype(BF16)
    r, k, v, lw, kk, a, g = _rwkv_prep(proj, mu_p, w0, w2_p, a0, a2_p, g2.astype(BF16), k_k, k_a)
    y_c = _rwkv_scan(r, k, v, lw, kk, a, g, r_k.reshape(-1), lnx_w, lnx_b)
    y_s = _s5(proj, RWKV_PC // LANES, s5_a_re, s5_a_im, s5_log_step, s5_b_re, s5_b_im, s5_c_re, s5_c_im,
              s5_d.reshape(-1))
    y_d = _matmul([y_s.astype(BF16)], [glu_w.astype(BF16)], BF16, epilogue="glu",
                  extra=(y_s, glu_b.reshape(1, -1)))
    wo = w_out.astype(BF16)
    return _matmul([y_c, y_d], [wo[:RWKV_WIDTH], wo[RWKV_WIDTH:]], F32, epilogue="residual", extra=(h,))


def kernel(x, norm_mix_w, norm_ffn_w, ev_w_in, ev_q_norm_w, ev_k_norm_w, ev_conv_w, ev_conv_b, ev_dt_bias, ev_a_log, ev_d_skip, ev_ssd_norm_w, ev_w_out, od_w_in, od_mu, od_w0, od_w2, od_a0, od_a2, od_g2, od_k_k, od_k_a, od_r_k, od_lnx_w, od_lnx_b, od_s5_a_re, od_s5_a_im, od_s5_log_step, od_s5_b_re, od_s5_b_im, od_s5_c_re, od_s5_c_im, od_s5_d, od_glu_w, od_glu_b, od_w_out, moe_wg, moe_bg, moe_we, moe_be, moe_w1, moe_w3, moe_w2):
    b, s, d = x.shape
    assert b == 1
    h = x.reshape(s, d)
    depth = norm_mix_w.shape[0]
    for layer in range(depth):
        i = layer // 2
        if layer % 2 == 0:
            h = _even_layer(h, norm_mix_w[layer], ev_w_in[i], ev_q_norm_w[i], ev_k_norm_w[i], ev_conv_w[i],
                            ev_conv_b[i], ev_dt_bias[i], ev_a_log[i], ev_d_skip[i], ev_ssd_norm_w[i], ev_w_out[i])
        else:
            h = _odd_layer(h, norm_mix_w[layer], od_w_in[i], od_mu[i], od_w0[i], od_w2[i], od_a0[i], od_a2[i],
                           od_g2[i], od_k_k[i], od_k_a[i], od_r_k[i], od_lnx_w[i], od_lnx_b[i], od_s5_a_re[i],
                           od_s5_a_im[i], od_s5_log_step[i], od_s5_b_re[i], od_s5_b_im[i], od_s5_c_re[i],
                           od_s5_c_im[i], od_s5_d[i], od_glu_w[i], od_glu_b[i], od_w_out[i])
        h = _moe(h, norm_ffn_w[layer], moe_wg[layer], moe_bg[layer], moe_we[layer], moe_be[layer],
                 moe_w1[layer].astype(BF16), moe_w3[layer].astype(BF16), moe_w2[layer].astype(BF16))
    return h.reshape(b, s, d)
```

```python
import functools
import math

import jax
import jax.numpy as jnp
from jax import lax
from jax.experimental import pallas as pl
from jax.experimental.pallas import tpu as pltpu

F32 = jnp.float32
BF16 = jnp.bfloat16

LANES = 128
SUBLANES = 8
VMEM_LIMIT_BYTES = 52 * 1024 * 1024

NORM_EPS = 1e-6
NEG_BIG = -1e30

ATT_HEADS = 16
ATT_HEAD_DIM = 128
ATT_WIDTH = ATT_HEADS * ATT_HEAD_DIM
DILATED_BRANCHES = ((128, 1), (512, 4), (2048, 16))
ATT_TILE = max(w for w, _ in DILATED_BRANCHES)
ROPE_THETA = 500000.0
ROPE_DIM = ATT_HEAD_DIM // 4
SSD_HEADS = 32
SSD_HEAD_DIM = 64
SSD_WIDTH = SSD_HEADS * SSD_HEAD_DIM
SSD_GROUPS = 8
SSD_STATE = 128
SSD_CONV = 4
SSD_CHUNK = 128
SSD_CONV_CH = SSD_WIDTH + 2 * SSD_GROUPS * SSD_STATE
EVEN_MAIN = 3 * ATT_WIDTH + SSD_WIDTH + SSD_CONV_CH
RWKV_HEADS = 32
RWKV_HEAD_DIM = 64
RWKV_WIDTH = RWKV_HEADS * RWKV_HEAD_DIM
DECAY_RANK = 96
ICL_RANK = 96
GATE_RANK = 256
RWKV_IN = 3 * RWKV_WIDTH + DECAY_RANK + ICL_RANK + GATE_RANK
RWKV_LORA = 512
RWKV_PC = 3 * RWKV_WIDTH + RWKV_LORA
RWKV_GN_EPS = 64e-5
RWKV_CHUNK = 64
S5_GROUP = 16
S5_GROUPS = 128
S5_WIDTH = S5_GROUP * S5_GROUPS
S5_STATE = 64
S5_GB = 8
S5_CH = S5_GB * S5_STATE
N_EXPERT_GROUPS = 4
EXPERTS_PER_GROUP = 8
N_EXPERTS = N_EXPERT_GROUPS * EXPERTS_PER_GROUP
TOP_K = 2
MOE_ROWS = 256


def _cparams(*sem):
    return pltpu.CompilerParams(dimension_semantics=sem, vmem_limit_bytes=VMEM_LIMIT_BYTES)


def _split3(x):
    hi = x.astype(BF16)
    r1 = x - hi.astype(F32)
    mid = r1.astype(BF16)
    lo = (r1 - mid.astype(F32)).astype(BF16)
    return hi, mid, lo


def _split2(x):
    hi = x.astype(BF16)
    return hi, (x - hi.astype(F32)).astype(BF16)


def _dot(a, b):
    return jnp.dot(a, b, preferred_element_type=F32)


def _dot_nt(a, b):
    return lax.dot_general(a, b, (((1,), (1,)), ((), ())), preferred_element_type=F32)


def _dot_tn(a, b):
    return lax.dot_general(a, b, (((0,), (0,)), ((), ())), preferred_element_type=F32)


def _dot_exact_lhs(l_bf16, x):
    hi, mid, lo = _split3(x)
    return _dot(l_bf16, hi) + _dot(l_bf16, mid) + _dot(l_bf16, lo)


def _dot_exact_rhs(x, r_bf16):
    hi, mid, lo = _split3(x)
    return _dot(hi, r_bf16) + _dot(mid, r_bf16) + _dot(lo, r_bf16)


def _dot_hi(a, b):
    a1, a2 = _split2(a)
    b1, b2 = _split2(b)
    return _dot(a1, b1) + (_dot(a1, b2) + _dot(a2, b1))


def _dot_hi_nt(a, b):
    a1, a2 = _split2(a)
    b1, b2 = _split2(b)
    return _dot_nt(a1, b1) + (_dot_nt(a1, b2) + _dot_nt(a2, b1))


def _sigmoid(x):
    return 1.0 / (1.0 + jnp.exp(-x))


def _softplus(x):
    return jnp.maximum(x, 0.0) + jnp.log(1.0 + jnp.exp(-jnp.abs(x)))


def _rmsnorm_kernel(x_ref, w_ref, o_ref):
    x = x_ref[...]
    ms = jnp.mean(x * x, axis=-1, keepdims=True)
    o_ref[...] = (x * lax.rsqrt(ms + NORM_EPS) * w_ref[...]).astype(o_ref.dtype)


def _rmsnorm(x, w, out_dtype, rows=256):
    t, d = x.shape
    rows = min(rows, t)
    return pl.pallas_call(
        _rmsnorm_kernel,
        grid=(t // rows,),
        in_specs=[pl.BlockSpec((rows, d), lambda i: (i, 0)),
                  pl.BlockSpec((1, d), lambda i: (0, 0))],
        out_specs=pl.BlockSpec((rows, d), lambda i: (i, 0)),
        out_shape=jax.ShapeDtypeStruct((t, d), out_dtype),
        compiler_params=_cparams("parallel"),
        name="rmsnorm",
    )(x, w.reshape(1, d))


def _mm_kernel(*refs, n_in, epilogue):
    a_refs = refs[:n_in]
    w_refs = refs[n_in:2 * n_in]
    wb_refs = refs[len(refs) - n_in:]
    rest = refs[2 * n_in:len(refs) - n_in]
    o_ref = rest[-1]

    @pl.when(pl.program_id(1) == 0)
    def _():
        for w_ref, wb_ref in zip(w_refs, wb_refs):
            wb_ref[...] = w_ref[...].astype(BF16)

    acc = _dot(a_refs[0][...], wb_refs[0][...])
    for a_ref, wb_ref in zip(a_refs[1:], wb_refs[1:]):
        acc = acc + _dot(a_ref[...], wb_ref[...])
    if epilogue == "residual":
        acc = rest[0][...] + acc
    elif epilogue == "glu":
        acc = rest[0][...] * _sigmoid(acc + rest[1][...])
    o_ref[...] = acc.astype(o_ref.dtype)


def _matmul(a_list, w, out_dtype, *, n, tm=512, tn=512, epilogue="none", extra=(), name="matmul"):
    m = a_list[0].shape[0]
    tm = min(tm, m)
    tn = min(tn, n)
    assert m % tm == 0 and n % tn == 0, (m, n, tm, tn)
    kw = a_list[0].shape[1]
    assert all(a.shape[1] == kw for a in a_list) and w.shape[0] == kw * len(a_list)
    in_specs = [pl.BlockSpec((tm, kw), lambda j, i: (i, 0)) for _ in a_list]
    in_specs += [pl.BlockSpec((kw, tn), functools.partial(lambda j, i, r: (r, j), r=r)) for r in range(len(a_list))]
    if epilogue == "residual":
        in_specs.append(pl.BlockSpec((tm, tn), lambda j, i: (i, j)))
    elif epilogue == "glu":
        in_specs.append(pl.BlockSpec((tm, tn), lambda j, i: (i, j)))
        in_specs.append(pl.BlockSpec((1, tn), lambda j, i: (0, j)))
    return pl.pallas_call(
        functools.partial(_mm_kernel, n_in=len(a_list), epilogue=epilogue),
        grid=(n // tn, m // tm),
        in_specs=in_specs,
        out_specs=pl.BlockSpec((tm, tn), lambda j, i: (i, j)),
        out_shape=jax.ShapeDtypeStruct((m, n), out_dtype),
        scratch_shapes=[pltpu.VMEM((kw, tn), BF16) for _ in a_list],
        compiler_params=_cparams("parallel", "arbitrary"),
        name=name,
    )(*a_list, *([w] * len(a_list)), *extra)


def _norm_rope(x, w, cos, sin):
    ms = jnp.mean(x * x, axis=-1, keepdims=True)
    y = x * lax.rsqrt(ms + NORM_EPS) * w
    half = ROPE_DIM // 2
    lane = lax.broadcasted_iota(jnp.int32, y.shape, 1)
    partner = jnp.where(lane < half, pltpu.roll(y, LANES - half, 1), pltpu.roll(y, half, 1))
    return y * cos + partner * sin


def _rows(start, size, stride):
    return pl.ds(start, size, stride=stride) if stride > 1 else pl.ds(start, size)


def _attn_kernel(q_ref, kc_ref, kp_ref, vc_ref, vp_ref, cosc_ref, sinc_ref, cosp_ref, sinp_ref, qw_ref, kw_ref,
                 o_ref, qs_ref, ks_ref, ob0, ob1, ob2, lb0, lb1, lb2):
    i = pl.program_id(1)
    tile = q_ref.shape[0]
    w = LANES
    scale = ATT_HEAD_DIM ** -0.5
    qs_ref[...] = _norm_rope(q_ref[...], qw_ref[...], cosc_ref[...], sinc_ref[...])
    ks_ref[pl.ds(0, tile), :] = _norm_rope(kp_ref[...], kw_ref[...], cosp_ref[...], sinp_ref[...])
    ks_ref[pl.ds(tile, tile), :] = _norm_rope(kc_ref[...], kw_ref[...], cosc_ref[...], sinc_ref[...])
    qi = lax.broadcasted_iota(jnp.int32, (w, w), 0)
    kj = lax.broadcasted_iota(jnp.int32, (w, w), 1)
    keep_c = kj <= qi
    keep_p = kj >= qi
    keep_p0 = jnp.logical_and(keep_p, i > 0)
    for (window, d), ob, lb in zip(DILATED_BRANCHES, (ob0, ob1, ob2), (lb0, lb1, lb2)):
        span = w * d
        assert span == window and tile % span == 0
        for nn in range(tile // span):
            for j in range(d):
                off = nn * span + j
                rq = _rows(off, w, d)
                q = qs_ref[rq, :].astype(BF16)
                kc = ks_ref[_rows(tile + off, w, d), :].astype(BF16)
                kp = ks_ref[_rows(tile + off - span, w, d), :].astype(BF16)
                vc = vc_ref[rq, :].astype(BF16)
                if nn == 0:
                    vp = vp_ref[_rows(tile - span + j, w, d), :].astype(BF16)
                else:
                    vp = vc_ref[_rows(off - span, w, d), :].astype(BF16)
                s_c = jnp.where(keep_c, _dot_nt(q, kc) * scale, NEG_BIG)
                s_p = jnp.where(keep_p0 if nn == 0 else keep_p, _dot_nt(q, kp) * scale, NEG_BIG)
                m = jnp.maximum(jnp.max(s_c, axis=-1, keepdims=True), jnp.max(s_p, axis=-1, keepdims=True))
                p_c = jnp.exp(s_c - m)
                p_p = jnp.exp(s_p - m)
                l = jnp.sum(p_c, axis=-1, keepdims=True) + jnp.sum(p_p, axis=-1, keepdims=True)
                o = _dot(p_c.astype(BF16), vc) + _dot(p_p.astype(BF16), vp)
                ob[rq, :] = o / l
                lb[rq, :] = jnp.broadcast_to(m + jnp.log(l), o.shape)
    a, b, c = lb0[...], lb1[...], lb2[...]
    m = jnp.maximum(jnp.maximum(a, b), c)
    wa, wb, wc = jnp.exp(a - m), jnp.exp(b - m), jnp.exp(c - m)
    o_ref[...] = ((wa * ob0[...] + wb * ob1[...] + wc * ob2[...]) / (wa + wb + wc)).astype(o_ref.dtype)


def _rope_tables(t):
    half = ROPE_DIM // 2
    inv_freq = ROPE_THETA ** (-jnp.arange(half, dtype=F32) / half)
    ang = jnp.arange(t, dtype=jnp.int32).astype(F32)[:, None] * inv_freq[None, :]
    cos, sin = jnp.cos(ang), jnp.sin(ang)
    ones = jnp.ones((t, ATT_HEAD_DIM - ROPE_DIM), F32)
    cos_f = jnp.concatenate([cos, cos, ones], axis=1)
    sin_f = jnp.concatenate([-sin, sin, 0.0 * ones], axis=1)
    return cos_f, sin_f


def _dilated_attention(proj, q_norm_w, k_norm_w):
    t = proj.shape[0]
    tile = ATT_TILE
    assert t % tile == 0
    cos_f, sin_f = _rope_tables(t)
    cur = lambda h, i: i
    prev = lambda h, i: jnp.maximum(i - 1, 0)
    blk = lambda rowf, col0: pl.BlockSpec((tile, LANES), lambda h, i: (rowf(h, i), col0 + h))
    tab = lambda rowf: pl.BlockSpec((tile, LANES), lambda h, i: (rowf(h, i), 0))
    const = pl.BlockSpec((1, LANES), lambda h, i: (0, 0))
    return pl.pallas_call(
        _attn_kernel,
        grid=(ATT_HEADS, t // tile),
        in_specs=[blk(cur, 0), blk(cur, ATT_HEADS), blk(prev, ATT_HEADS), blk(cur, 2 * ATT_HEADS),
                  blk(prev, 2 * ATT_HEADS), tab(cur), tab(cur), tab(prev), tab(prev), const, const],
        out_specs=pl.BlockSpec((tile, LANES), lambda h, i: (i, h)),
        out_shape=jax.ShapeDtypeStruct((t, ATT_WIDTH), BF16),
        scratch_shapes=[pltpu.VMEM((tile, LANES), F32), pltpu.VMEM((2 * tile, LANES), F32)]
        + [pltpu.VMEM((tile, LANES), F32)] * 6,
        compiler_params=_cparams("parallel", "arbitrary"),
        name="dilated_attention",
    )(proj, proj, proj, proj, proj, cos_f, sin_f, cos_f, sin_f,
      q_norm_w.reshape(1, LANES), k_norm_w.reshape(1, LANES))


def _ssd_kernel(z_ref, xbc_ref, xbcp_ref, dt_ref, dtt_ref, convw_ref, convb_ref, dtb_ref, dtbt_ref,
                alog_ref, alogt_ref, dskip_ref, normw_ref, o_ref, state_ref, y_ref):
    c = pl.program_id(0)
    q = SSD_CHUNK

    @pl.when(c == 0)
    def _():
        state_ref[...] = jnp.zeros_like(state_ref)

    cur = xbc_ref[...]
    prev = xbcp_ref[...] * jnp.where(c > 0, 1.0, 0.0)
    row = lax.broadcasted_iota(jnp.int32, cur.shape, 0)
    acc = convb_ref[...] + convw_ref[SSD_CONV - 1:SSD_CONV, :] * cur
    for k in range(1, SSD_CONV):
        sh = jnp.where(row < k, pltpu.roll(prev, k, 0), pltpu.roll(cur, k, 0))
        acc = acc + convw_ref[SSD_CONV - 1 - k:SSD_CONV - k, :] * sh
    xbc = acc * _sigmoid(acc)
    xs = xbc[:, :SSD_WIDTH]
    gn = SSD_GROUPS * SSD_STATE
    bm = xbc[:, SSD_WIDTH:SSD_WIDTH + gn].astype(BF16)
    cm = xbc[:, SSD_WIDTH + gn:].astype(BF16)

    li = lax.broadcasted_iota(jnp.int32, (q, q), 0)
    si = lax.broadcasted_iota(jnp.int32, (q, q), 1)
    causal = li >= si
    tri = jnp.where(causal, 1.0, 0.0).astype(BF16)
    tri_t = jnp.where(li <= si, 1.0, 0.0).astype(BF16)

    dt = _softplus(dt_ref[...] + dtb_ref[...])
    acs = _dot_exact_lhs(tri, dt * (-jnp.exp(alog_ref[...])))
    dt_t = _softplus(dtt_ref[...] + dtbt_ref[...])
    acs_t = _dot_exact_rhs(dt_t * (-jnp.exp(alogt_ref[...])), tri_t)

    epg = SSD_HEADS // SSD_GROUPS
    for g in range(SSD_GROUPS):
        bg = bm[:, g * SSD_STATE:(g + 1) * SSD_STATE]
        cg = cm[:, g * SSD_STATE:(g + 1) * SSD_STATE]
        cb = _dot_nt(cg, bg)
        for e in range(epg):
            h = g * epg + e
            col = acs[:, h:h + 1]
            rowv = acs_t[h:h + 1, :]
            last = acs[q - 1:q, h:h + 1]
            decay = jnp.exp(jnp.where(causal, col - rowv, NEG_BIG))
            xh = xs[:, h * SSD_HEAD_DIM:(h + 1) * SSD_HEAD_DIM]
            xdt = xh * dt[:, h:h + 1]
            hstate = state_ref[h]
            y = _dot((cb * decay).astype(BF16), xdt.astype(BF16))
            y = y + _dot_nt(cg, hstate.astype(BF16)) * jnp.exp(col)
            st = _dot_tn((xdt * jnp.exp(last - col)).astype(BF16), bg)
            state_ref[h] = hstate * jnp.exp(last) + st
            y_ref[:, h * SSD_HEAD_DIM:(h + 1) * SSD_HEAD_DIM] = y
    z = z_ref[...]
    y = (y_ref[...] + dskip_ref[...] * xs) * (z * _sigmoid(z))
    ms = jnp.mean(y * y, axis=-1, keepdims=True)
    o_ref[...] = (y * lax.rsqrt(ms + NORM_EPS) * normw_ref[...]).astype(o_ref.dtype)


def _mamba2(proj, dt_pad, conv_w, conv_b, dt_bias, a_log, d_skip, norm_w):
    t = proj.shape[0]
    q = SSD_CHUNK
    nc = t // q
    pad = LANES - SSD_HEADS
    dtb = jnp.pad(dt_bias, (0, pad)).reshape(1, LANES)
    alog = jnp.pad(a_log, (0, pad)).reshape(1, LANES)
    dskip = jnp.repeat(d_skip, SSD_HEAD_DIM).reshape(1, SSD_WIDTH)
    z_blk = (3 * ATT_WIDTH) // SSD_WIDTH
    xbc_blk = (3 * ATT_WIDTH + SSD_WIDTH) // SSD_CONV_CH
    assert z_blk * SSD_WIDTH == 3 * ATT_WIDTH and xbc_blk * SSD_CONV_CH == 3 * ATT_WIDTH + SSD_WIDTH
    const = lambda shape: pl.BlockSpec(shape, lambda c: (0,) * len(shape))
    return pl.pallas_call(
        _ssd_kernel,
        grid=(nc,),
        in_specs=[
            pl.BlockSpec((q, SSD_WIDTH), lambda c: (c, z_blk)),
            pl.BlockSpec((q, SSD_CONV_CH), lambda c: (c, xbc_blk)),
            pl.BlockSpec((q, SSD_CONV_CH), lambda c: (jnp.maximum(c - 1, 0), xbc_blk)),
            pl.BlockSpec((q, LANES), lambda c: (c, 0)),
            pl.BlockSpec((LANES, q), lambda c: (0, c)),
            const((SSD_CONV, SSD_CONV_CH)), const((1, SSD_CONV_CH)),
            const((1, LANES)), const((LANES, 1)), const((1, LANES)), const((LANES, 1)),
            const((1, SSD_WIDTH)), const((1, SSD_WIDTH)),
        ],
        out_specs=pl.BlockSpec((q, SSD_WIDTH), lambda c: (c, 0)),
        out_shape=jax.ShapeDtypeStruct((t, SSD_WIDTH), BF16),
        scratch_shapes=[pltpu.VMEM((SSD_HEADS, SSD_HEAD_DIM, SSD_STATE), F32),
                        pltpu.VMEM((q, SSD_WIDTH), F32)],
        compiler_params=_cparams("arbitrary"),
        name="mamba2_ssd",
    )(proj, proj, proj, dt_pad, dt_pad.T, conv_w, conv_b.reshape(1, -1), dtb, dtb.reshape(LANES, 1),
      alog, alog.reshape(LANES, 1), dskip, norm_w.reshape(1, -1))


def _rwkv_prep_kernel(pc_ref, pp_ref, mu_ref, w0_ref, w2_ref, a0_ref, a2_ref, g2_ref, kk_ref, ka_ref,
                      hsum_ref, hexp_ref, r_o, k_o, v_o, lw_o, kk_o, a_o, g_o):
    i = pl.program_id(0)
    cur = pc_ref[...]
    row = lax.broadcasted_iota(jnp.int32, cur.shape, 0)
    last_prev = pp_ref[SUBLANES - 1:SUBLANES, :] * jnp.where(i > 0, 1.0, 0.0)
    prev = jnp.where(row == 0, last_prev, pltpu.roll(cur, 1, 0))
    p = cur + (prev - cur) * mu_ref[...]
    w = RWKV_WIDTH
    r, k, v = p[:, :w], p[:, w:2 * w], p[:, 2 * w:3 * w]
    lora = p[:, 3 * w:]
    w_log = -_softplus(-(w0_ref[...] + _dot(jnp.tanh(lora).astype(BF16), w2_ref[...]))) - 0.5
    a = _sigmoid(a0_ref[...] + _dot(lora.astype(BF16), a2_ref[...]))
    g = _dot(_sigmoid(lora).astype(BF16), g2_ref[...])
    kkr = k * kk_ref[...]
    ss = _dot_exact_rhs(kkr * kkr, hsum_ref[...])
    nrm = jnp.maximum(jnp.sqrt(ss), 1e-12)
    inv = _dot_exact_rhs(1.0 / nrm, hexp_ref[...])
    r_o[...] = r
    k_o[...] = k * (1.0 + (a - 1.0) * ka_ref[...])
    v_o[...] = v
    lw_o[...] = -jnp.exp(w_log)
    kk_o[...] = kkr * inv
    a_o[...] = a
    g_o[...] = g


def _place_rows(m, start, total):
    return jnp.pad(m, ((start, total - start - m.shape[0]), (0, 0))).astype(BF16)


def _rwkv_prep(pc, mu, w0, w2, a0, a2, g2, k_k, k_a, rows=128):
    t = pc.shape[0]
    rows = min(rows, t)
    w = RWKV_WIDTH
    mu_p = jnp.pad(mu, (0, RWKV_PC - RWKV_IN)).reshape(1, RWKV_PC)
    w2_p = _place_rows(w2, 0, RWKV_LORA)
    a2_p = _place_rows(a2, DECAY_RANK, RWKV_LORA)
    g2_p = _place_rows(g2, DECAY_RANK + ICL_RANK, RWKV_LORA)
    head_of = jnp.arange(w) // RWKV_HEAD_DIM
    hsum = (head_of[:, None] == jnp.arange(LANES)[None, :]).astype(BF16)
    hexp = hsum.T
    const = lambda shape: pl.BlockSpec(shape, lambda i: (0,) * len(shape))
    rb = rows // SUBLANES
    return pl.pallas_call(
        _rwkv_prep_kernel,
        grid=(t // rows,),
        in_specs=[pl.BlockSpec((rows, RWKV_PC), lambda i: (i, 0)),
                  pl.BlockSpec((SUBLANES, RWKV_PC), lambda i: (jnp.maximum(i * rb - 1, 0), 0)),
                  const((1, RWKV_PC)), const((1, w)), const((RWKV_LORA, w)), const((1, w)),
                  const((RWKV_LORA, w)), const((RWKV_LORA, w)), const((1, w)), const((1, w)),
                  const((w, LANES)), const((LANES, w))],
        out_specs=[pl.BlockSpec((rows, w), lambda i: (i, 0))] * 7,
        out_shape=[jax.ShapeDtypeStruct((t, w), F32)] * 7,
        compiler_params=_cparams("parallel"),
        name="rwkv_prep",
    )(pc, pc, mu_p, w0.reshape(1, w), w2_p, a0.reshape(1, w), a2_p, g2_p, k_k.reshape(1, w),
      k_a.reshape(1, w), hsum, hexp)


def _rwkv_scan_kernel(r_ref, k_ref, v_ref, lw_ref, kk_ref, a_ref, g_ref, rk_ref, lnw_ref, lnb_ref,
                      o_ref, state_ref, *, heads_per_step):
    ci = pl.program_id(1)
    c = RWKV_CHUNK
    n = RWKV_HEAD_DIM

    @pl.when(ci == 0)
    def _():
        state_ref[...] = jnp.zeros_like(state_ref)

    ti = lax.broadcasted_iota(jnp.int32, (c, c), 0)
    tj = lax.broadcasted_iota(jnp.int32, (c, c), 1)
    tri = jnp.where(ti >= tj, 1.0, 0.0).astype(BF16)
    strict = ti > tj
    incl = ti >= tj
    eye = jnp.where(ti == tj, 1.0, 0.0)

    hs = range(heads_per_step)
    sls = [slice(hh * n, (hh + 1) * n) for hh in hs]
    cum = _dot_exact_lhs(tri, lw_ref[...])
    lw_all = lw_ref[...]
    w_incl_all = jnp.exp(cum)
    w_inv_all = jnp.exp(-cum)
    at_all = -kk_ref[...] * jnp.exp(cum - lw_all)
    bt_all = kk_ref[...] * a_ref[...] * w_inv_all
    rt_all = (r_ref[...] * w_incl_all).astype(BF16)
    kt_all = (k_ref[...] * w_inv_all).astype(BF16)
    v_all = v_ref[...].astype(BF16)
    at = [at_all[:, sl] for sl in sls]
    bt = [bt_all[:, sl] for sl in sls]
    rt = [rt_all[:, sl] for sl in sls]
    kt = [kt_all[:, sl] for sl in sls]
    vb = [v_all[:, sl] for sl in sls]
    atb = [x.astype(BF16) for x in at]
    btb = [x.astype(BF16) for x in bt]
    s0 = [state_ref[hh] for hh in hs]
    s0b = [x.astype(BF16) for x in s0]
    a_ab = [jnp.where(strict, _dot_hi_nt(at[h], bt[h]), 0.0) for h in hs]
    a_ak = [jnp.where(strict, _dot_nt(atb[h], kt[h]), 0.0).astype(BF16) for h in hs]
    a_rb = [jnp.where(incl, _dot_nt(rt[h], btb[h]), 0.0).astype(BF16) for h in hs]
    a_rk = [jnp.where(incl, _dot_nt(rt[h], kt[h]), 0.0).astype(BF16) for h in hs]
    rhs = [_dot_nt(atb[h], s0b[h]) + _dot(a_ak[h], vb[h]) for h in hs]
    tinv = [eye + x for x in a_ab]
    pw = a_ab
    for _ in range(int(math.log2(c)) - 1):
        pw = [_dot_hi(x, x) for x in pw]
        tinv = [t + _dot_hi(t, x) for t, x in zip(tinv, pw)]
    ub = [_dot_hi(tinv[h], rhs[h]).astype(BF16) for h in hs]
    y = [_dot_nt(rt[h], s0b[h]) + _dot(a_rb[h], ub[h]) + _dot(a_rk[h], vb[h]) for h in hs]
    for h in hs:
        state_ref[h] = (s0[h] + _dot_tn(ub[h], btb[h]) + _dot_tn(vb[h], kt[h])) * w_incl_all[c - 1:c, sls[h]]
    rk_prod = r_ref[...] * k_ref[...] * rk_ref[...]
    for h in hs:
        sl = sls[h]
        mean = jnp.mean(y[h], axis=-1, keepdims=True)
        var = jnp.mean(jnp.square(y[h] - mean), axis=-1, keepdims=True)
        yn = (y[h] - mean) * lax.rsqrt(var + RWKV_GN_EPS) * lnw_ref[:, sl] + lnb_ref[:, sl]
        bonus = jnp.sum(rk_prod[:, sl], axis=-1, keepdims=True) * v_ref[:, sl]
        o_ref[:, sl] = ((yn + bonus) * g_ref[:, sl]).astype(o_ref.dtype)


def _rwkv_scan(r, k, v, lw, kk, a, g, r_k, lnx_w, lnx_b, heads_per_step=8):
    t, w = r.shape
    c = RWKV_CHUNK
    hb = heads_per_step
    wb = hb * RWKV_HEAD_DIM
    seq = pl.BlockSpec((c, wb), lambda h, i: (i, h))
    par = pl.BlockSpec((1, wb), lambda h, i: (0, h))
    return pl.pallas_call(
        functools.partial(_rwkv_scan_kernel, heads_per_step=hb),
        grid=(RWKV_HEADS // hb, t // c),
        in_specs=[seq] * 7 + [par] * 3,
        out_specs=seq,
        out_shape=jax.ShapeDtypeStruct((t, w), BF16),
        scratch_shapes=[pltpu.VMEM((hb, RWKV_HEAD_DIM, RWKV_HEAD_DIM), F32)],
        compiler_params=_cparams("parallel", "arbitrary"),
        name="rwkv_scan",
    )(r, k, v, lw, kk, a, g, r_k.reshape(1, w), lnx_w.reshape(1, w), lnx_b.reshape(1, w))


def _s5_kernel(u_ref, bb_ref, cd_ref, tab_ref, d_ref, o_ref, xs_ref, carry_ref):
    tt = pl.program_id(1)
    rows = u_ref.shape[0]
    nch = S5_CH

    @pl.when(tt == 0)
    def _():
        carry_ref[...] = jnp.zeros_like(carry_ref)

    u = u_ref[...]
    xs_ref[...] = _dot(u.astype(BF16), bb_ref[0])
    tab = tab_ref[0]
    steps = ((1, tab[0], tab[1]), (2, tab[2], tab[3]), (4, tab[4], tab[5]))
    pr, pi = tab[6], tab[7]

    def body(i, carry):
        cr, ci = carry
        base = pl.multiple_of(i * SUBLANES, SUBLANES)
        xr = xs_ref[pl.ds(base, SUBLANES), :nch]
        xi = xs_ref[pl.ds(base, SUBLANES), nch:]
        for s, mr, mi in steps:
            rr = pltpu.roll(xr, s, 0)
            ri = pltpu.roll(xi, s, 0)
            xr, xi = xr + mr * rr - mi * ri, xi + mr * ri + mi * rr
        xr, xi = xr + pr * cr - pi * ci, xi + pr * ci + pi * cr
        xs_ref[pl.ds(base, SUBLANES), :nch] = xr
        xs_ref[pl.ds(base, SUBLANES), nch:] = xi
        return xr[SUBLANES - 1:SUBLANES, :], xi[SUBLANES - 1:SUBLANES, :]

    cr, ci = lax.fori_loop(0, rows // SUBLANES, body, (carry_ref[0:1, :], carry_ref[1:2, :]))
    carry_ref[0:1, :] = cr
    carry_ref[1:2, :] = ci
    y = _dot(xs_ref[...].astype(BF16), cd_ref[0]) + d_ref[...] * u
    o_ref[...] = 0.5 * y * (1.0 + jnp.tanh(math.sqrt(2.0 / math.pi) * (y + 0.044715 * (y * y * y))))


def _s5_params(a_re, a_im, log_step, b_re, b_im, c_re, c_im):
    lam_re = jnp.minimum(a_re, -1e-4)
    lam_im = a_im
    step = jnp.exp(log_step)[:, None]
    mag = jnp.exp(lam_re * step)
    lb_re = mag * jnp.cos(lam_im * step)
    lb_im = mag * jnp.sin(lam_im * step)
    den = lam_re * lam_re + lam_im * lam_im
    f_re = ((lb_re - 1.0) * lam_re + lb_im * lam_im) / den
    f_im = (lb_im * lam_re - (lb_re - 1.0) * lam_im) / den
    bb_re = f_re[..., None] * b_re - f_im[..., None] * b_im
    bb_im = f_re[..., None] * b_im + f_im[..., None] * b_re
    nb = S5_GROUPS // S5_GB
    eye = jnp.eye(S5_GB, dtype=F32)

    def bdiag_in(m):
        mt = m.transpose(0, 2, 1).reshape(nb, S5_GB, S5_GROUP, S5_STATE)
        return jnp.einsum('bghp,gk->bghkp', mt, eye).reshape(nb, S5_GB * S5_GROUP, S5_CH)

    def bdiag_out(m):
        mt = m.transpose(0, 2, 1).reshape(nb, S5_GB, S5_STATE, S5_GROUP)
        return jnp.einsum('bgph,gk->bgpkh', mt, eye).reshape(nb, S5_CH, S5_GB * S5_GROUP)

    bb = jnp.concatenate([bdiag_in(bb_re), bdiag_in(bb_im)], axis=2).astype(BF16)
    cd = jnp.concatenate([bdiag_out(c_re), -bdiag_out(c_im)], axis=1).astype(BF16)
    pr, pi = [lb_re], [lb_im]
    for _ in range(SUBLANES - 1):
        pr, pi = pr + [pr[-1] * lb_re - pi[-1] * lb_im], pi + [pr[-1] * lb_im + pi[-1] * lb_re]
    pw_r = jnp.stack(pr, axis=0).reshape(SUBLANES, nb, S5_CH).transpose(1, 0, 2)
    pw_i = jnp.stack(pi, axis=0).reshape(SUBLANES, nb, S5_CH).transpose(1, 0, 2)
    rowi = jnp.arange(SUBLANES)[None, :, None]
    kinds = []
    for s in (1, 2, 4):
        kinds.append(jnp.where(rowi >= s, pw_r[:, s - 1:s, :], 0.0))
        kinds.append(jnp.where(rowi >= s, pw_i[:, s - 1:s, :], 0.0))
    kinds += [pw_r, pw_i]
    tab = jnp.stack(kinds, axis=1)
    return bb, cd, tab


def _s5(u, a_re, a_im, log_step, b_re, b_im, c_re, c_im, d_skip, rows=512):
    t = u.shape[0]
    rows = min(rows, t)
    nb = S5_GROUPS // S5_GB
    bb, cd, tab = _s5_params(a_re, a_im, log_step, b_re, b_im, c_re, c_im)
    return pl.pallas_call(
        _s5_kernel,
        grid=(nb, t // rows),
        in_specs=[pl.BlockSpec((rows, LANES), lambda b, i: (i, b)),
                  pl.BlockSpec((1, LANES, 2 * S5_CH), lambda b, i: (b, 0, 0)),
                  pl.BlockSpec((1, 2 * S5_CH, LANES), lambda b, i: (b, 0, 0)),
                  pl.BlockSpec((1, 8, SUBLANES, S5_CH), lambda b, i: (b, 0, 0, 0)),
                  pl.BlockSpec((1, LANES), lambda b, i: (0, b))],
        out_specs=pl.BlockSpec((rows, LANES), lambda b, i: (i, b)),
        out_shape=jax.ShapeDtypeStruct((t, S5_WIDTH), F32),
        scratch_shapes=[pltpu.VMEM((rows, 2 * S5_CH), F32), pltpu.VMEM((SUBLANES, S5_CH), F32)],
        compiler_params=_cparams("parallel", "arbitrary"),
        name="s5_ssm",
    )(u, bb, cd, tab, d_skip.reshape(1, S5_WIDTH))


def _router_kernel(h_ref, nw_ref, wr_ref, br_ref, x_ref, idx_ref, wt_ref):
    h = h_ref[...]
    ms = jnp.mean(h * h, axis=-1, keepdims=True)
    x = h * lax.rsqrt(ms + NORM_EPS) * nw_ref[...]
    x_ref[...] = x
    xh, xm = _split2(x)
    logits = (_dot(xh, wr_ref[0]) + (_dot(xm, wr_ref[0]) + _dot(xh, wr_ref[1]))) + br_ref[...]
    lane = lax.broadcasted_iota(jnp.int32, logits.shape, 1)
    big = jnp.int32(4 * LANES)
    ng, epg = N_EXPERT_GROUPS, EXPERTS_PER_GROUP
    lg = jnp.where(lane < ng, logits, NEG_BIG)
    mg = jnp.max(lg, axis=-1, keepdims=True)
    g_sel = jnp.min(jnp.where(lg == mg, lane, big), axis=-1, keepdims=True)
    pg_top = 1.0 / jnp.sum(jnp.exp(lg - mg), axis=-1, keepdims=True)
    lo = ng + g_sel * epg
    le = jnp.where(jnp.logical_and(lane >= lo, lane < lo + epg), logits, NEG_BIG)
    m1 = jnp.max(le, axis=-1, keepdims=True)
    i1 = jnp.min(jnp.where(le == m1, lane, big), axis=-1, keepdims=True)
    zsum = jnp.sum(jnp.exp(le - m1), axis=-1, keepdims=True)
    le2 = jnp.where(lane == i1, NEG_BIG, le)
    m2 = jnp.max(le2, axis=-1, keepdims=True)
    i2 = jnp.min(jnp.where(le2 == m2, lane, big), axis=-1, keepdims=True)
    p1 = 1.0 / zsum
    p2 = jnp.exp(m2 - m1) / zsum
    c1 = pg_top * p1 / (p1 + p2)
    c2 = pg_top * p2 / (p1 + p2)
    idx_ref[...] = jnp.where(lane == 0, i1 - ng, jnp.where(lane == 1, i2 - ng, 0))
    wt_ref[...] = jnp.where(lane == 0, c1, jnp.where(lane == 1, c2, 0.0))


def _router(h, norm_w, wg, bg, we, be, rows=256):
    t, d = h.shape
    rows = min(rows, t)
    ncol = N_EXPERT_GROUPS + N_EXPERTS
    wr = jnp.pad(jnp.concatenate([wg, we], axis=1), ((0, 0), (0, LANES - ncol)))
    wr2 = jnp.stack(_split2(wr))
    br = jnp.pad(jnp.concatenate([bg, be]), (0, LANES - ncol)).reshape(1, LANES)
    return pl.pallas_call(
        _router_kernel,
        grid=(t // rows,),
        in_specs=[pl.BlockSpec((rows, d), lambda i: (i, 0)),
                  pl.BlockSpec((1, d), lambda i: (0, 0)),
                  pl.BlockSpec((2, d, LANES), lambda i: (0, 0, 0)),
                  pl.BlockSpec((1, LANES), lambda i: (0, 0))],
        out_specs=[pl.BlockSpec((rows, d), lambda i: (i, 0)),
                   pl.BlockSpec((rows, LANES), lambda i: (i, 0)),
                   pl.BlockSpec((rows, LANES), lambda i: (i, 0))],
        out_shape=[jax.ShapeDtypeStruct((t, d), F32),
                   jax.ShapeDtypeStruct((t, LANES), jnp.int32),
                   jax.ShapeDtypeStruct((t, LANES), F32)],
        compiler_params=_cparams("parallel"),
        name="moe_router",
    )(h, norm_w.reshape(1, d), wr2, br)


def _row_copy(src_hbm, dst_ref, sem, src_row, dst_row):
    return pltpu.make_async_copy(src_hbm.at[pl.ds(src_row, 1)], dst_ref.at[pl.ds(dst_row, 1)], sem)


def _gather_rows_kernel(tok_ref, nv_ref, x_hbm, o_ref, buf_ref, sem, *, rows):
    b = pl.program_id(0)
    base = b * rows

    @pl.when(nv_ref[b] > 0)
    def _():
        def start(i, _):
            _row_copy(x_hbm, buf_ref, sem, tok_ref[base + i], i).start()
            return 0

        def wait(i, _):
            _row_copy(x_hbm, buf_ref, sem, tok_ref[base + i], i).wait()
            return 0

        lax.fori_loop(0, rows, start, 0)
        lax.fori_loop(0, rows, wait, 0)
        o_ref[...] = buf_ref[...].astype(o_ref.dtype)

    @pl.when(nv_ref[b] == 0)
    def _():
        o_ref[...] = jnp.zeros_like(o_ref)


def _gather_rows(x, tok, nvalid, rows):
    n = tok.shape[0]
    d = x.shape[1]
    return pl.pallas_call(
        functools.partial(_gather_rows_kernel, rows=rows),
        grid_spec=pltpu.PrefetchScalarGridSpec(
            num_scalar_prefetch=2,
            grid=(n // rows,),
            in_specs=[pl.BlockSpec(memory_space=pl.ANY)],
            out_specs=pl.BlockSpec((rows, d), lambda b, tok, nv: (b, 0)),
            scratch_shapes=[pltpu.VMEM((rows, d), x.dtype), pltpu.SemaphoreType.DMA],
        ),
        out_shape=jax.ShapeDtypeStruct((n, d), BF16),
        compiler_params=_cparams("arbitrary"),
        name="moe_gather",
    )(tok, nvalid, x)


def _expert_up_kernel(be_ref, nv_ref, chg_ref, x_ref, w1_ref, w3_ref, h_ref, w1b_ref, w3b_ref):
    b = pl.program_id(1)

    @pl.when(chg_ref[b] == 1)
    def _():
        w1b_ref[...] = w1_ref[0, 0].astype(BF16)
        w3b_ref[...] = w3_ref[0, 0].astype(BF16)

    @pl.when(nv_ref[b] > 0)
    def _():
        x = x_ref[...]
        h1 = _dot(x, w1b_ref[...])
        h3 = _dot(x, w3b_ref[...])
        h_ref[...] = (h1 * _sigmoid(h1) * h3).astype(h_ref.dtype)

    @pl.when(nv_ref[b] == 0)
    def _():
        h_ref[...] = jnp.zeros_like(h_ref)


def _expert_down_kernel(be_ref, nv_ref, chg_ref, h_ref, w2_ref, o_ref, w2b_ref):
    b = pl.program_id(1)

    @pl.when(chg_ref[b] == 1)
    def _():
        w2b_ref[...] = w2_ref[0, 0].astype(BF16)

    @pl.when(nv_ref[b] > 0)
    def _():
        o_ref[...] = _dot(h_ref[...], w2b_ref[...])

    @pl.when(nv_ref[b] == 0)
    def _():
        o_ref[...] = jnp.zeros_like(o_ref)


def _experts(xg, block_e, nvalid, changed, w1, w3, w2, layer, ff_tile=512, out_tile=1024):
    n, d = xg.shape
    rows = MOE_ROWS
    ff = w1.shape[3]
    ff_tile = min(ff_tile, ff)
    out_tile = min(out_tile, d)
    nb = n // rows
    hbuf = pl.pallas_call(
        _expert_up_kernel,
        grid_spec=pltpu.PrefetchScalarGridSpec(
            num_scalar_prefetch=3,
            grid=(ff // ff_tile, nb),
            in_specs=[pl.BlockSpec((rows, d), lambda f, b, be, nv, ch: (b, 0)),
                      pl.BlockSpec((1, 1, d, ff_tile), lambda f, b, be, nv, ch: (layer, be[b], 0, f)),
                      pl.BlockSpec((1, 1, d, ff_tile), lambda f, b, be, nv, ch: (layer, be[b], 0, f))],
            out_specs=pl.BlockSpec((rows, ff_tile), lambda f, b, be, nv, ch: (b, f)),
            scratch_shapes=[pltpu.VMEM((d, ff_tile), BF16), pltpu.VMEM((d, ff_tile), BF16)],
        ),
        out_shape=jax.ShapeDtypeStruct((n, ff), BF16),
        compiler_params=_cparams("arbitrary", "arbitrary"),
        name="moe_expert_up",
    )(block_e, nvalid, changed, xg, w1, w3)
    return pl.pallas_call(
        _expert_down_kernel,
        grid_spec=pltpu.PrefetchScalarGridSpec(
            num_scalar_prefetch=3,
            grid=(d // out_tile, nb),
            in_specs=[pl.BlockSpec((rows, ff), lambda j, b, be, nv, ch: (b, 0)),
                      pl.BlockSpec((1, 1, ff, out_tile), lambda j, b, be, nv, ch: (layer, be[b], 0, j))],
            out_specs=pl.BlockSpec((rows, out_tile), lambda j, b, be, nv, ch: (b, j)),
            scratch_shapes=[pltpu.VMEM((ff, out_tile), BF16)],
        ),
        out_shape=jax.ShapeDtypeStruct((n, d), F32),
        compiler_params=_cparams("arbitrary", "arbitrary"),
        name="moe_expert_down",
    )(block_e, nvalid, changed, hbuf, w2)


def _combine_kernel(dest_ref, h_ref, wt_ref, yb_hbm, o_ref, buf_ref, sem, *, rows):
    base = pl.program_id(0) * rows

    def start(i, _):
        for j in range(TOP_K):
            _row_copy(yb_hbm, buf_ref.at[j], sem, dest_ref[(base + i) * TOP_K + j], i).start()
        return 0

    def wait(i, _):
        for j in range(TOP_K):
            _row_copy(yb_hbm, buf_ref.at[j], sem, dest_ref[(base + i) * TOP_K + j], i).wait()
        return 0

    lax.fori_loop(0, rows, start, 0)
    lax.fori_loop(0, rows, wait, 0)
    wt = wt_ref[...]
    o_ref[...] = h_ref[...] + wt[:, 0:1] * buf_ref[0] + wt[:, 1:2] * buf_ref[1]


def _combine(h, wt, yb, dest, rows=128):
    t, d = h.shape
    rows = min(rows, t)
    return pl.pallas_call(
        functools.partial(_combine_kernel, rows=rows),
        grid_spec=pltpu.PrefetchScalarGridSpec(
            num_scalar_prefetch=1,
            grid=(t // rows,),
            in_specs=[pl.BlockSpec((rows, d), lambda i, dr: (i, 0)),
                      pl.BlockSpec((rows, LANES), lambda i, dr: (i, 0)),
                      pl.BlockSpec(memory_space=pl.ANY)],
            out_specs=pl.BlockSpec((rows, d), lambda i, dr: (i, 0)),
            scratch_shapes=[pltpu.VMEM((TOP_K, rows, d), F32), pltpu.SemaphoreType.DMA],
        ),
        out_shape=jax.ShapeDtypeStruct((t, d), F32),
        compiler_params=_cparams("arbitrary"),
        name="moe_combine",
    )(dest, h, wt, yb)


def _routing_tables(expert_idx, rows):
    t = expert_idx.shape[0]
    a = t * TOP_K
    e_n = N_EXPERTS
    i32 = jnp.int32
    flat_e = expert_idx.reshape(-1)
    onehot = (flat_e[:, None] == jnp.arange(e_n, dtype=i32)[None, :]).astype(i32)
    counts = jnp.sum(onehot, axis=0)
    rank = jnp.take_along_axis(jnp.cumsum(onehot, axis=0), flat_e[:, None], axis=1)[:, 0] - 1
    padded = (counts + rows - 1) // rows * rows
    pad_end = jnp.cumsum(padded)
    pad_start = pad_end - padded
    dest = (pad_start[flat_e] + rank).astype(i32)
    n_blocks = a // rows + e_n
    blk_start = jnp.arange(n_blocks, dtype=i32) * rows
    block_e = jnp.minimum(jnp.sum((pad_end[None, :] <= blk_start[:, None]).astype(i32), axis=1), e_n - 1)
    nvalid = jnp.clip(counts[block_e] - (blk_start - pad_start[block_e]), 0, rows).astype(i32)
    changed = jnp.concatenate([jnp.ones((1,), i32), (block_e[1:] != block_e[:-1]).astype(i32)])
    slot_tok = jnp.zeros((n_blocks * rows,), i32).at[dest].set(jnp.arange(a, dtype=i32) // TOP_K)
    return slot_tok, dest, block_e.astype(i32), nvalid, changed


def _moe(h, norm_w, wg, bg, we, be, w1, w3, w2, layer):
    x, idx, wt = _router(h, norm_w, wg, bg, we, be)
    slot_tok, dest, block_e, nvalid, changed = _routing_tables(idx[:, :TOP_K], MOE_ROWS)
    xg = _gather_rows(x, slot_tok, nvalid, MOE_ROWS)
    yb = _experts(xg, block_e, nvalid, changed, w1, w3, w2, layer)
    return _combine(h, wt, yb, dest)


def _even_layer(h, norm_w, w_in, q_norm_w, k_norm_w, conv_w, conv_b, dt_bias, a_log, d_skip, ssd_norm_w, w_out):
    xn = _rmsnorm(h, norm_w, BF16)
    proj = _matmul([xn], w_in, F32, n=EVEN_MAIN, name="even_in_proj")
    w_dt = jnp.pad(w_in[:, EVEN_MAIN:], ((0, 0), (0, LANES - SSD_HEADS)))
    dt_pad = _matmul([xn], w_dt, F32, n=LANES, name="even_dt_proj")
    y_att = _dilated_attention(proj, q_norm_w, k_norm_w)
    y_ssd = _mamba2(proj, dt_pad, conv_w, conv_b, dt_bias, a_log, d_skip, ssd_norm_w)
    return _matmul([y_att, y_ssd], w_out, F32, n=w_out.shape[1], epilogue="residual", extra=(h,),
                   name="even_out_proj")


def _odd_layer(h, norm_w, w_in, mu, w0, w2, a0, a2, g2, k_k, k_a, r_k, lnx_w, lnx_b,
               s5_a_re, s5_a_im, s5_log_step, s5_b_re, s5_b_im, s5_c_re, s5_c_im, s5_d,
               glu_w, glu_b, w_out):
    xn = _rmsnorm(h, norm_w, BF16)
    pc = _matmul([xn], w_in, F32, n=RWKV_PC, name="odd_in_proj_rwkv")
    u = _matmul([xn], w_in[:, RWKV_IN:], F32, n=S5_WIDTH, name="odd_in_proj_s5")
    r, k, v, lw, kk, a, g = _rwkv_prep(pc, mu, w0, w2, a0, a2, g2, k_k, k_a)
    y_c = _rwkv_scan(r, k, v, lw, kk, a, g, r_k.reshape(-1), lnx_w, lnx_b)
    y_s = _s5(u, s5_a_re, s5_a_im, s5_log_step, s5_b_re, s5_b_im, s5_c_re, s5_c_im, s5_d.reshape(-1))
    y_d = _matmul([y_s.astype(BF16)], glu_w, BF16, n=S5_WIDTH, epilogue="glu",
                  extra=(y_s, glu_b.reshape(1, -1)), name="s5_glu")
    return _matmul([y_c, y_d], w_out, F32, n=w_out.shape[1], epilogue="residual", extra=(h,),
                   name="odd_out_proj")


def kernel(x, norm_mix_w, norm_ffn_w, ev_w_in, ev_q_norm_w, ev_k_norm_w, ev_conv_w, ev_conv_b, ev_dt_bias, ev_a_log, ev_d_skip, ev_ssd_norm_w, ev_w_out, od_w_in, od_mu, od_w0, od_w2, od_a0, od_a2, od_g2, od_k_k, od_k_a, od_r_k, od_lnx_w, od_lnx_b, od_s5_a_re, od_s5_a_im, od_s5_log_step, od_s5_b_re, od_s5_b_im, od_s5_c_re, od_s5_c_im, od_s5_d, od_glu_w, od_glu_b, od_w_out, moe_wg, moe_bg, moe_we, moe_be, moe_w1, moe_w3, moe_w2):
    b, s, d = x.shape
    assert b == 1
    h = x.reshape(s, d)
    depth = norm_mix_w.shape[0]
    for layer in range(depth):
        i = layer // 2
        if layer % 2 == 0:
            h = _even_layer(h, norm_mix_w[layer], ev_w_in[i], ev_q_norm_w[i], ev_k_norm_w[i], ev_conv_w[i],
                            ev_conv_b[i], ev_dt_bias[i], ev_a_log[i], ev_d_skip[i], ev_ssd_norm_w[i], ev_w_out[i])
        else:
            h = _odd_layer(h, norm_mix_w[layer], od_w_in[i], od_mu[i], od_w0[i], od_w2[i], od_a0[i], od_a2[i],
                           od_g2[i], od_k_k[i], od_k_a[i], od_r_k[i], od_lnx_w[i], od_lnx_b[i], od_s5_a_re[i],
                           od_s5_a_im[i], od_s5_log_step[i], od_s5_b_re[i], od_s5_b_im[i], od_s5_c_re[i],
                           od_s5_c_im[i], od_s5_d[i], od_glu_w[i], od_glu_b[i], od_w_out[i])
        h = _moe(h, norm_ffn_w[layer], moe_wg[layer], moe_bg[layer], moe_we[layer], moe_be[layer],
                 moe_w1, moe_w3, moe_w2, layer)
    return h.reshape(b, s, d)
```

```python
import functools
import math

import jax
import jax.numpy as jnp
from jax import lax
from jax.experimental import pallas as pl
from jax.experimental.pallas import tpu as pltpu

F32 = jnp.float32
BF16 = jnp.bfloat16

LANES = 128
SUBLANES = 8
VMEM_LIMIT_BYTES = 52 * 1024 * 1024

NORM_EPS = 1e-6
NEG_BIG = -1e30

ATT_HEADS = 16
ATT_HEAD_DIM = 128
ATT_WIDTH = ATT_HEADS * ATT_HEAD_DIM
DILATED_BRANCHES = ((128, 1), (512, 4), (2048, 16))
ATT_TILE = max(w for w, _ in DILATED_BRANCHES)
ROPE_THETA = 500000.0
ROPE_DIM = ATT_HEAD_DIM // 4
SSD_HEADS = 32
SSD_HEAD_DIM = 64
SSD_WIDTH = SSD_HEADS * SSD_HEAD_DIM
SSD_GROUPS = 8
SSD_STATE = 128
SSD_CONV = 4
SSD_CHUNK = 128
SSD_CONV_CH = SSD_WIDTH + 2 * SSD_GROUPS * SSD_STATE
EVEN_MAIN = 3 * ATT_WIDTH + SSD_WIDTH + SSD_CONV_CH
RWKV_HEADS = 32
RWKV_HEAD_DIM = 64
RWKV_WIDTH = RWKV_HEADS * RWKV_HEAD_DIM
DECAY_RANK = 96
ICL_RANK = 96
GATE_RANK = 256
RWKV_IN = 3 * RWKV_WIDTH + DECAY_RANK + ICL_RANK + GATE_RANK
RWKV_LORA = 512
RWKV_PC = 3 * RWKV_WIDTH + RWKV_LORA
RWKV_GN_EPS = 64e-5
RWKV_CHUNK = 64
S5_GROUP = 16
S5_GROUPS = 128
S5_WIDTH = S5_GROUP * S5_GROUPS
S5_STATE = 64
S5_GB = 8
S5_CH = S5_GB * S5_STATE
N_EXPERT_GROUPS = 4
EXPERTS_PER_GROUP = 8
N_EXPERTS = N_EXPERT_GROUPS * EXPERTS_PER_GROUP
TOP_K = 2
MOE_ROWS = 256


def _cparams(*sem):
    return pltpu.CompilerParams(dimension_semantics=sem, vmem_limit_bytes=VMEM_LIMIT_BYTES)


def _split3(x):
    hi = x.astype(BF16)
    r1 = x - hi.astype(F32)
    mid = r1.astype(BF16)
    lo = (r1 - mid.astype(F32)).astype(BF16)
    return hi, mid, lo


def _split2(x):
    hi = x.astype(BF16)
    return hi, (x - hi.astype(F32)).astype(BF16)


def _dot(a, b):
    return jnp.dot(a, b, preferred_element_type=F32)


def _dot_nt(a, b):
    return lax.dot_general(a, b, (((1,), (1,)), ((), ())), preferred_element_type=F32)


def _dot_tn(a, b):
    return lax.dot_general(a, b, (((0,), (0,)), ((), ())), preferred_element_type=F32)


def _dot_exact_lhs(l_bf16, x):
    hi, mid, lo = _split3(x)
    return _dot(l_bf16, hi) + _dot(l_bf16, mid) + _dot(l_bf16, lo)


def _dot_exact_rhs(x, r_bf16):
    hi, mid, lo = _split3(x)
    return _dot(hi, r_bf16) + _dot(mid, r_bf16) + _dot(lo, r_bf16)


def _dot_hi_nt(a, b):
    a1, a2 = _split2(a)
    b1, b2 = _split2(b)
    return _dot_nt(a1, b1) + (_dot_nt(a1, b2) + _dot_nt(a2, b1))


def _sigmoid(x):
    return 1.0 / (1.0 + jnp.exp(-x))


def _softplus(x):
    return jnp.maximum(x, 0.0) + jnp.log(1.0 + jnp.exp(-jnp.abs(x)))


def _rmsnorm_kernel(x_ref, w_ref, o_ref):
    x = x_ref[...]
    ms = jnp.mean(x * x, axis=-1, keepdims=True)
    o_ref[...] = (x * lax.rsqrt(ms + NORM_EPS) * w_ref[...]).astype(o_ref.dtype)


def _rmsnorm(x, w, out_dtype, rows=256):
    t, d = x.shape
    rows = min(rows, t)
    return pl.pallas_call(
        _rmsnorm_kernel,
        grid=(t // rows,),
        in_specs=[pl.BlockSpec((rows, d), lambda i: (i, 0)),
                  pl.BlockSpec((1, d), lambda i: (0, 0))],
        out_specs=pl.BlockSpec((rows, d), lambda i: (i, 0)),
        out_shape=jax.ShapeDtypeStruct((t, d), out_dtype),
        compiler_params=_cparams("parallel"),
        name="rmsnorm",
    )(x, w.reshape(1, d))


def _mm_kernel(*refs, n_in, epilogue, w_transposed):
    a_refs = refs[:n_in]
    w_refs = refs[n_in:2 * n_in]
    wb_refs = refs[len(refs) - n_in:]
    rest = refs[2 * n_in:len(refs) - n_in]
    o_ref = rest[-1]
    dot = _dot_nt if w_transposed else _dot

    @pl.when(pl.program_id(1) == 0)
    def _():
        for w_ref, wb_ref in zip(w_refs, wb_refs):
            wb_ref[...] = w_ref[...].astype(BF16)

    acc = dot(a_refs[0][...], wb_refs[0][...])
    for a_ref, wb_ref in zip(a_refs[1:], wb_refs[1:]):
        acc = acc + dot(a_ref[...], wb_ref[...])
    if epilogue == "residual":
        acc = rest[0][...] + acc
    elif epilogue == "glu":
        acc = rest[0][...] * _sigmoid(acc + rest[1][...])
    o_ref[...] = acc.astype(o_ref.dtype)


def _matmul(a_list, w, out_dtype, *, n, tm=1024, tn=512, epilogue="none", extra=(), name="matmul",
            w_transposed=False):
    m = a_list[0].shape[0]
    tm = min(tm, m)
    tn = min(tn, n)
    assert m % tm == 0 and n % tn == 0, (m, n, tm, tn)
    kw = a_list[0].shape[1]
    assert all(a.shape[1] == kw for a in a_list) and w.shape[1 if w_transposed else 0] == kw * len(a_list)
    in_specs = [pl.BlockSpec((tm, kw), lambda j, i: (i, 0)) for _ in a_list]
    if w_transposed:
        w_block = (tn, kw)
        in_specs += [pl.BlockSpec(w_block, functools.partial(lambda j, i, r: (j, r), r=r)) for r in range(len(a_list))]
    else:
        w_block = (kw, tn)
        in_specs += [pl.BlockSpec(w_block, functools.partial(lambda j, i, r: (r, j), r=r)) for r in range(len(a_list))]
    if epilogue == "residual":
        in_specs.append(pl.BlockSpec((tm, tn), lambda j, i: (i, j)))
    elif epilogue == "glu":
        in_specs.append(pl.BlockSpec((tm, tn), lambda j, i: (i, j)))
        in_specs.append(pl.BlockSpec((1, tn), lambda j, i: (0, j)))
    return pl.pallas_call(
        functools.partial(_mm_kernel, n_in=len(a_list), epilogue=epilogue, w_transposed=w_transposed),
        grid=(n // tn, m // tm),
        in_specs=in_specs,
        out_specs=pl.BlockSpec((tm, tn), lambda j, i: (i, j)),
        out_shape=jax.ShapeDtypeStruct((m, n), out_dtype),
        scratch_shapes=[pltpu.VMEM(w_block, BF16) for _ in a_list],
        compiler_params=_cparams("parallel", "arbitrary"),
        name=name,
    )(*a_list, *([w] * len(a_list)), *extra)


def _norm_rope(x, w, cos, sin):
    ms = jnp.mean(x * x, axis=-1, keepdims=True)
    y = x * lax.rsqrt(ms + NORM_EPS) * w
    half = ROPE_DIM // 2
    lane = lax.broadcasted_iota(jnp.int32, y.shape, 1)
    partner = jnp.where(lane < half, pltpu.roll(y, LANES - half, 1), pltpu.roll(y, half, 1))
    return y * cos + partner * sin


def _rows(start, size, stride):
    return pl.ds(start, size, stride=stride) if stride > 1 else pl.ds(start, size)


def _attn_kernel(q_ref, kc_ref, kp_ref, vc_ref, vp_ref, cosc_ref, sinc_ref, cosp_ref, sinp_ref, qw_ref, kw_ref,
                 o_ref, qs_ref, ks_ref, ob0, ob1, ob2, lb0, lb1, lb2):
    i = pl.program_id(1)
    tile = q_ref.shape[0]
    w = LANES
    scale = ATT_HEAD_DIM ** -0.5
    qs_ref[...] = _norm_rope(q_ref[...], qw_ref[...], cosc_ref[...], sinc_ref[...])
    ks_ref[pl.ds(0, tile), :] = _norm_rope(kp_ref[...], kw_ref[...], cosp_ref[...], sinp_ref[...])
    ks_ref[pl.ds(tile, tile), :] = _norm_rope(kc_ref[...], kw_ref[...], cosc_ref[...], sinc_ref[...])
    qi = lax.broadcasted_iota(jnp.int32, (w, w), 0)
    kj = lax.broadcasted_iota(jnp.int32, (w, w), 1)
    keep_c = kj <= qi
    keep_p = kj >= qi
    keep_p0 = jnp.logical_and(keep_p, i > 0)
    for (window, d), ob, lb in zip(DILATED_BRANCHES, (ob0, ob1, ob2), (lb0, lb1, lb2)):
        span = w * d
        assert span == window and tile % span == 0
        for nn in range(tile // span):
            for j in range(d):
                off = nn * span + j
                rq = _rows(off, w, d)
                q = qs_ref[rq, :].astype(BF16)
                kc = ks_ref[_rows(tile + off, w, d), :].astype(BF16)
                kp = ks_ref[_rows(tile + off - span, w, d), :].astype(BF16)
                vc = vc_ref[rq, :].astype(BF16)
                if nn == 0:
                    vp = vp_ref[_rows(tile - span + j, w, d), :].astype(BF16)
                else:
                    vp = vc_ref[_rows(off - span, w, d), :].astype(BF16)
                s_c = jnp.where(keep_c, _dot_nt(q, kc) * scale, NEG_BIG)
                s_p = jnp.where(keep_p0 if nn == 0 else keep_p, _dot_nt(q, kp) * scale, NEG_BIG)
                m = jnp.maximum(jnp.max(s_c, axis=-1, keepdims=True), jnp.max(s_p, axis=-1, keepdims=True))
                p_c = jnp.exp(s_c - m)
                p_p = jnp.exp(s_p - m)
                l = jnp.sum(p_c, axis=-1, keepdims=True) + jnp.sum(p_p, axis=-1, keepdims=True)
                o = _dot(p_c.astype(BF16), vc) + _dot(p_p.astype(BF16), vp)
                ob[rq, :] = o / l
                lb[rq, :] = jnp.broadcast_to(m + jnp.log(l), o.shape)
    a, b, c = lb0[...], lb1[...], lb2[...]
    m = jnp.maximum(jnp.maximum(a, b), c)
    wa, wb, wc = jnp.exp(a - m), jnp.exp(b - m), jnp.exp(c - m)
    o_ref[...] = ((wa * ob0[...] + wb * ob1[...] + wc * ob2[...]) / (wa + wb + wc)).astype(o_ref.dtype)


def _rope_tables(t):
    half = ROPE_DIM // 2
    inv_freq = ROPE_THETA ** (-jnp.arange(half, dtype=F32) / half)
    ang = jnp.arange(t, dtype=jnp.int32).astype(F32)[:, None] * inv_freq[None, :]
    cos, sin = jnp.cos(ang), jnp.sin(ang)
    ones = jnp.ones((t, ATT_HEAD_DIM - ROPE_DIM), F32)
    cos_f = jnp.concatenate([cos, cos, ones], axis=1)
    sin_f = jnp.concatenate([-sin, sin, 0.0 * ones], axis=1)
    return cos_f, sin_f


def _dilated_attention(proj, q_norm_w, k_norm_w):
    t = proj.shape[0]
    tile = ATT_TILE
    assert t % tile == 0
    cos_f, sin_f = _rope_tables(t)
    cur = lambda h, i: i
    prev = lambda h, i: jnp.maximum(i - 1, 0)
    blk = lambda rowf, col0: pl.BlockSpec((tile, LANES), lambda h, i: (rowf(h, i), col0 + h))
    tab = lambda rowf: pl.BlockSpec((tile, LANES), lambda h, i: (rowf(h, i), 0))
    const = pl.BlockSpec((1, LANES), lambda h, i: (0, 0))
    return pl.pallas_call(
        _attn_kernel,
        grid=(ATT_HEADS, t // tile),
        in_specs=[blk(cur, 0), blk(cur, ATT_HEADS), blk(prev, ATT_HEADS), blk(cur, 2 * ATT_HEADS),
                  blk(prev, 2 * ATT_HEADS), tab(cur), tab(cur), tab(prev), tab(prev), const, const],
        out_specs=pl.BlockSpec((tile, LANES), lambda h, i: (i, h)),
        out_shape=jax.ShapeDtypeStruct((t, ATT_WIDTH), BF16),
        scratch_shapes=[pltpu.VMEM((tile, LANES), F32), pltpu.VMEM((2 * tile, LANES), F32)]
        + [pltpu.VMEM((tile, LANES), F32)] * 6,
        compiler_params=_cparams("parallel", "arbitrary"),
        name="dilated_attention",
    )(proj, proj, proj, proj, proj, cos_f, sin_f, cos_f, sin_f,
      q_norm_w.reshape(1, LANES), k_norm_w.reshape(1, LANES))


def _ssd_kernel(z_ref, xbc_ref, xbcp_ref, dt_ref, dtt_ref, convw_ref, convb_ref, dtb_ref, dtbt_ref,
                alog_ref, alogt_ref, dskip_ref, normw_ref, o_ref, state_ref, y_ref):
    c = pl.program_id(0)
    q = SSD_CHUNK

    @pl.when(c == 0)
    def _():
        state_ref[...] = jnp.zeros_like(state_ref)

    cur = xbc_ref[...]
    prev = xbcp_ref[...] * jnp.where(c > 0, 1.0, 0.0)
    row = lax.broadcasted_iota(jnp.int32, cur.shape, 0)
    acc = convb_ref[...] + convw_ref[SSD_CONV - 1:SSD_CONV, :] * cur
    for k in range(1, SSD_CONV):
        sh = jnp.where(row < k, pltpu.roll(prev, k, 0), pltpu.roll(cur, k, 0))
        acc = acc + convw_ref[SSD_CONV - 1 - k:SSD_CONV - k, :] * sh
    xbc = acc * _sigmoid(acc)
    xs = xbc[:, :SSD_WIDTH]
    gn = SSD_GROUPS * SSD_STATE
    bm = xbc[:, SSD_WIDTH:SSD_WIDTH + gn].astype(BF16)
    cm = xbc[:, SSD_WIDTH + gn:].astype(BF16)

    li = lax.broadcasted_iota(jnp.int32, (q, q), 0)
    si = lax.broadcasted_iota(jnp.int32, (q, q), 1)
    causal = li >= si
    tri = jnp.where(causal, 1.0, 0.0).astype(BF16)
    tri_t = jnp.where(li <= si, 1.0, 0.0).astype(BF16)

    dt = _softplus(dt_ref[...] + dtb_ref[...])
    acs = _dot_exact_lhs(tri, dt * (-jnp.exp(alog_ref[...])))
    dt_t = _softplus(dtt_ref[...] + dtbt_ref[...])
    acs_t = _dot_exact_rhs(dt_t * (-jnp.exp(alogt_ref[...])), tri_t)

    epg = SSD_HEADS // SSD_GROUPS
    for g in range(SSD_GROUPS):
        bg = bm[:, g * SSD_STATE:(g + 1) * SSD_STATE]
        cg = cm[:, g * SSD_STATE:(g + 1) * SSD_STATE]
        cb = _dot_nt(cg, bg)
        for e in range(epg):
            h = g * epg + e
            col = acs[:, h:h + 1]
            rowv = acs_t[h:h + 1, :]
            last = acs[q - 1:q, h:h + 1]
            decay = jnp.exp(jnp.where(causal, col - rowv, NEG_BIG))
            xh = xs[:, h * SSD_HEAD_DIM:(h + 1) * SSD_HEAD_DIM]
            xdt = xh * dt[:, h:h + 1]
            hstate = state_ref[h]
            y = _dot((cb * decay).astype(BF16), xdt.astype(BF16))
            y = y + _dot_nt(cg, hstate.astype(BF16)) * jnp.exp(col)
            st = _dot_tn((xdt * jnp.exp(last - col)).astype(BF16), bg)
            state_ref[h] = hstate * jnp.exp(last) + st
            y_ref[:, h * SSD_HEAD_DIM:(h + 1) * SSD_HEAD_DIM] = y
    z = z_ref[...]
    y = (y_ref[...] + dskip_ref[...] * xs) * (z * _sigmoid(z))
    ms = jnp.mean(y * y, axis=-1, keepdims=True)
    o_ref[...] = (y * lax.rsqrt(ms + NORM_EPS) * normw_ref[...]).astype(o_ref.dtype)


def _mamba2(proj, dt_pad, conv_w, conv_b, dt_bias, a_log, d_skip, norm_w):
    t = proj.shape[0]
    q = SSD_CHUNK
    nc = t // q
    pad = LANES - SSD_HEADS
    dtb = jnp.pad(dt_bias, (0, pad)).reshape(1, LANES)
    alog = jnp.pad(a_log, (0, pad)).reshape(1, LANES)
    dskip = jnp.repeat(d_skip, SSD_HEAD_DIM).reshape(1, SSD_WIDTH)
    z_blk = (3 * ATT_WIDTH) // SSD_WIDTH
    xbc_blk = (3 * ATT_WIDTH + SSD_WIDTH) // SSD_CONV_CH
    assert z_blk * SSD_WIDTH == 3 * ATT_WIDTH and xbc_blk * SSD_CONV_CH == 3 * ATT_WIDTH + SSD_WIDTH
    const = lambda shape: pl.BlockSpec(shape, lambda c: (0,) * len(shape))
    return pl.pallas_call(
        _ssd_kernel,
        grid=(nc,),
        in_specs=[
            pl.BlockSpec((q, SSD_WIDTH), lambda c: (c, z_blk)),
            pl.BlockSpec((q, SSD_CONV_CH), lambda c: (c, xbc_blk)),
            pl.BlockSpec((q, SSD_CONV_CH), lambda c: (jnp.maximum(c - 1, 0), xbc_blk)),
            pl.BlockSpec((q, LANES), lambda c: (c, 0)),
            pl.BlockSpec((LANES, q), lambda c: (0, c)),
            const((SSD_CONV, SSD_CONV_CH)), const((1, SSD_CONV_CH)),
            const((1, LANES)), const((LANES, 1)), const((1, LANES)), const((LANES, 1)),
            const((1, SSD_WIDTH)), const((1, SSD_WIDTH)),
        ],
        out_specs=pl.BlockSpec((q, SSD_WIDTH), lambda c: (c, 0)),
        out_shape=jax.ShapeDtypeStruct((t, SSD_WIDTH), BF16),
        scratch_shapes=[pltpu.VMEM((SSD_HEADS, SSD_HEAD_DIM, SSD_STATE), F32),
                        pltpu.VMEM((q, SSD_WIDTH), F32)],
        compiler_params=_cparams("arbitrary"),
        name="mamba2_ssd",
    )(proj, proj, proj, dt_pad, dt_pad.T, conv_w, conv_b.reshape(1, -1), dtb, dtb.reshape(LANES, 1),
      alog, alog.reshape(LANES, 1), dskip, norm_w.reshape(1, -1))


def _rwkv_prep_kernel(pc_ref, pp_ref, mu_ref, w0_ref, w2_ref, a0_ref, a2_ref, g2_ref, kk_ref, ka_ref,
                      hsum_ref, hexp_ref, r_o, k_o, v_o, lw_o, kk_o, a_o, g_o):
    i = pl.program_id(0)
    cur = pc_ref[...]
    row = lax.broadcasted_iota(jnp.int32, cur.shape, 0)
    last_prev = pp_ref[SUBLANES - 1:SUBLANES, :] * jnp.where(i > 0, 1.0, 0.0)
    prev = jnp.where(row == 0, last_prev, pltpu.roll(cur, 1, 0))
    p = cur + (prev - cur) * mu_ref[...]
    w = RWKV_WIDTH
    r, k, v = p[:, :w], p[:, w:2 * w], p[:, 2 * w:3 * w]
    lora = p[:, 3 * w:]
    w_log = -_softplus(-(w0_ref[...] + _dot(jnp.tanh(lora).astype(BF16), w2_ref[...]))) - 0.5
    a = _sigmoid(a0_ref[...] + _dot(lora.astype(BF16), a2_ref[...]))
    g = _dot(_sigmoid(lora).astype(BF16), g2_ref[...])
    kkr = k * kk_ref[...]
    ss = _dot_exact_rhs(kkr * kkr, hsum_ref[...])
    nrm = jnp.maximum(jnp.sqrt(ss), 1e-12)
    inv = _dot_exact_rhs(1.0 / nrm, hexp_ref[...])
    r_o[...] = r
    k_o[...] = k * (1.0 + (a - 1.0) * ka_ref[...])
    v_o[...] = v
    lw_o[...] = -jnp.exp(w_log)
    kk_o[...] = kkr * inv
    a_o[...] = a
    g_o[...] = g


def _place_rows(m, start, total):
    return jnp.pad(m, ((start, total - start - m.shape[0]), (0, 0))).astype(BF16)


def _rwkv_prep(pc, mu, w0, w2, a0, a2, g2, k_k, k_a, rows=128):
    t = pc.shape[0]
    rows = min(rows, t)
    w = RWKV_WIDTH
    mu_p = jnp.pad(mu, (0, RWKV_PC - RWKV_IN)).reshape(1, RWKV_PC)
    w2_p = _place_rows(w2, 0, RWKV_LORA)
    a2_p = _place_rows(a2, DECAY_RANK, RWKV_LORA)
    g2_p = _place_rows(g2, DECAY_RANK + ICL_RANK, RWKV_LORA)
    head_of = jnp.arange(w) // RWKV_HEAD_DIM
    hsum = (head_of[:, None] == jnp.arange(LANES)[None, :]).astype(BF16)
    hexp = hsum.T
    const = lambda shape: pl.BlockSpec(shape, lambda i: (0,) * len(shape))
    rb = rows // SUBLANES
    return pl.pallas_call(
        _rwkv_prep_kernel,
        grid=(t // rows,),
        in_specs=[pl.BlockSpec((rows, RWKV_PC), lambda i: (i, 0)),
                  pl.BlockSpec((SUBLANES, RWKV_PC), lambda i: (jnp.maximum(i * rb - 1, 0), 0)),
                  const((1, RWKV_PC)), const((1, w)), const((RWKV_LORA, w)), const((1, w)),
                  const((RWKV_LORA, w)), const((RWKV_LORA, w)), const((1, w)), const((1, w)),
                  const((w, LANES)), const((LANES, w))],
        out_specs=[pl.BlockSpec((rows, w), lambda i: (i, 0))] * 7,
        out_shape=[jax.ShapeDtypeStruct((t, w), F32)] * 7,
        compiler_params=_cparams("parallel"),
        name="rwkv_prep",
    )(pc, pc, mu_p, w0.reshape(1, w), w2_p, a0.reshape(1, w), a2_p, g2_p, k_k.reshape(1, w),
      k_a.reshape(1, w), hsum, hexp)


def _rwkv_scan_kernel(r_ref, k_ref, v_ref, lw_ref, kk_ref, a_ref, g_ref, rk_ref, lnw_ref, lnb_ref,
                      o_ref, state_ref, *, heads_per_step):
    ci = pl.program_id(1)
    c = RWKV_CHUNK
    n = RWKV_HEAD_DIM

    @pl.when(ci == 0)
    def _():
        state_ref[...] = jnp.zeros_like(state_ref)

    ti = lax.broadcasted_iota(jnp.int32, (c, c), 0)
    tj = lax.broadcasted_iota(jnp.int32, (c, c), 1)
    tri = jnp.where(ti >= tj, 1.0, 0.0).astype(BF16)
    strict = ti > tj
    incl = ti >= tj
    eye = jnp.where(ti == tj, 1.0, 0.0)

    hs = range(heads_per_step)
    sls = [slice(hh * n, (hh + 1) * n) for hh in hs]
    cum = _dot_exact_lhs(tri, lw_ref[...])
    lw_all = lw_ref[...]
    w_incl_all = jnp.exp(cum)
    w_inv_all = jnp.exp(-cum)
    at_all = -kk_ref[...] * jnp.exp(cum - lw_all)
    bt_all = kk_ref[...] * a_ref[...] * w_inv_all
    rt_all = (r_ref[...] * w_incl_all).astype(BF16)
    kt_all = (k_ref[...] * w_inv_all).astype(BF16)
    v_all = v_ref[...].astype(BF16)
    at = [at_all[:, sl] for sl in sls]
    bt = [bt_all[:, sl] for sl in sls]
    rt = [rt_all[:, sl] for sl in sls]
    kt = [kt_all[:, sl] for sl in sls]
    vb = [v_all[:, sl] for sl in sls]
    atb = [x.astype(BF16) for x in at]
    btb = [x.astype(BF16) for x in bt]
    s0 = [state_ref[hh] for hh in hs]
    s0b = [x.astype(BF16) for x in s0]
    a_ab = [jnp.where(strict, _dot_hi_nt(at[h], bt[h]), 0.0) for h in hs]
    a_ak = [jnp.where(strict, _dot_nt(atb[h], kt[h]), 0.0).astype(BF16) for h in hs]
    a_rb = [jnp.where(incl, _dot_nt(rt[h], btb[h]), 0.0).astype(BF16) for h in hs]
    a_rk = [jnp.where(incl, _dot_nt(rt[h], kt[h]), 0.0).astype(BF16) for h in hs]
    rhs = [_dot_nt(atb[h], s0b[h]) + _dot(a_ak[h], vb[h]) for h in hs]
    tinv = [eye + x for x in a_ab]
    pw = [x.astype(BF16) for x in a_ab]
    for _ in range(int(math.log2(c)) - 1):
        pw = [_dot(x, x).astype(BF16) for x in pw]
        tinv = [t + _dot(t.astype(BF16), x) for t, x in zip(tinv, pw)]
    ub = [_dot(tinv[h].astype(BF16), rhs[h].astype(BF16)).astype(BF16) for h in hs]
    y = [_dot_nt(rt[h], s0b[h]) + _dot(a_rb[h], ub[h]) + _dot(a_rk[h], vb[h]) for h in hs]
    for h in hs:
        state_ref[h] = (s0[h] + _dot_tn(ub[h], btb[h]) + _dot_tn(vb[h], kt[h])) * w_incl_all[c - 1:c, sls[h]]
    rk_prod = r_ref[...] * k_ref[...] * rk_ref[...]
    for h in hs:
        sl = sls[h]
        mean = jnp.mean(y[h], axis=-1, keepdims=True)
        var = jnp.mean(jnp.square(y[h] - mean), axis=-1, keepdims=True)
        yn = (y[h] - mean) * lax.rsqrt(var + RWKV_GN_EPS) * lnw_ref[:, sl] + lnb_ref[:, sl]
        bonus = jnp.sum(rk_prod[:, sl], axis=-1, keepdims=True) * v_ref[:, sl]
        o_ref[:, sl] = ((yn + bonus) * g_ref[:, sl]).astype(o_ref.dtype)


def _rwkv_scan(r, k, v, lw, kk, a, g, r_k, lnx_w, lnx_b, heads_per_step=16):
    t, w = r.shape
    c = RWKV_CHUNK
    hb = heads_per_step
    wb = hb * RWKV_HEAD_DIM
    seq = pl.BlockSpec((c, wb), lambda h, i: (i, h))
    par = pl.BlockSpec((1, wb), lambda h, i: (0, h))
    return pl.pallas_call(
        functools.partial(_rwkv_scan_kernel, heads_per_step=hb),
        grid=(RWKV_HEADS // hb, t // c),
        in_specs=[seq] * 7 + [par] * 3,
        out_specs=seq,
        out_shape=jax.ShapeDtypeStruct((t, w), BF16),
        scratch_shapes=[pltpu.VMEM((hb, RWKV_HEAD_DIM, RWKV_HEAD_DIM), F32)],
        compiler_params=_cparams("parallel", "arbitrary"),
        name="rwkv_scan",
    )(r, k, v, lw, kk, a, g, r_k.reshape(1, w), lnx_w.reshape(1, w), lnx_b.reshape(1, w))


def _s5_kernel(u_ref, bb_ref, cd_ref, tab_ref, d_ref, o_ref, xs_ref, carry_ref):
    tt = pl.program_id(1)
    rows = u_ref.shape[0]
    nch = S5_CH

    @pl.when(tt == 0)
    def _():
        carry_ref[...] = jnp.zeros_like(carry_ref)

    u = u_ref[...]
    xs_ref[...] = _dot(u.astype(BF16), bb_ref[0])
    tab = tab_ref[0]
    steps = ((1, tab[0], tab[1]), (2, tab[2], tab[3]), (4, tab[4], tab[5]))
    pr, pi = tab[6], tab[7]

    def body(i, carry):
        cr, ci = carry
        base = pl.multiple_of(i * SUBLANES, SUBLANES)
        xr = xs_ref[pl.ds(base, SUBLANES), :nch]
        xi = xs_ref[pl.ds(base, SUBLANES), nch:]
        for s, mr, mi in steps:
            rr = pltpu.roll(xr, s, 0)
            ri = pltpu.roll(xi, s, 0)
            xr, xi = xr + mr * rr - mi * ri, xi + mr * ri + mi * rr
        xr, xi = xr + pr * cr - pi * ci, xi + pr * ci + pi * cr
        xs_ref[pl.ds(base, SUBLANES), :nch] = xr
        xs_ref[pl.ds(base, SUBLANES), nch:] = xi
        return xr[SUBLANES - 1:SUBLANES, :], xi[SUBLANES - 1:SUBLANES, :]

    cr, ci = lax.fori_loop(0, rows // SUBLANES, body, (carry_ref[0:1, :], carry_ref[1:2, :]))
    carry_ref[0:1, :] = cr
    carry_ref[1:2, :] = ci
    y = _dot(xs_ref[...].astype(BF16), cd_ref[0]) + d_ref[...] * u
    o_ref[...] = 0.5 * y * (1.0 + jnp.tanh(math.sqrt(2.0 / math.pi) * (y + 0.044715 * (y * y * y))))


def _s5_params(a_re, a_im, log_step, b_re, b_im, c_re, c_im):
    lam_re = jnp.minimum(a_re, -1e-4)
    lam_im = a_im
    step = jnp.exp(log_step)[:, None]
    mag = jnp.exp(lam_re * step)
    lb_re = mag * jnp.cos(lam_im * step)
    lb_im = mag * jnp.sin(lam_im * step)
    den = lam_re * lam_re + lam_im * lam_im
    f_re = ((lb_re - 1.0) * lam_re + lb_im * lam_im) / den
    f_im = (lb_im * lam_re - (lb_re - 1.0) * lam_im) / den
    bb_re = f_re[..., None] * b_re - f_im[..., None] * b_im
    bb_im = f_re[..., None] * b_im + f_im[..., None] * b_re
    nb = S5_GROUPS // S5_GB
    eye = jnp.eye(S5_GB, dtype=F32)

    def bdiag_in(m):
        mt = m.transpose(0, 2, 1).reshape(nb, S5_GB, S5_GROUP, S5_STATE)
        return jnp.einsum('bghp,gk->bghkp', mt, eye).reshape(nb, S5_GB * S5_GROUP, S5_CH)

    def bdiag_out(m):
        mt = m.transpose(0, 2, 1).reshape(nb, S5_GB, S5_STATE, S5_GROUP)
        return jnp.einsum('bgph,gk->bgpkh', mt, eye).reshape(nb, S5_CH, S5_GB * S5_GROUP)

    bb = jnp.concatenate([bdiag_in(bb_re), bdiag_in(bb_im)], axis=2).astype(BF16)
    cd = jnp.concatenate([bdiag_out(c_re), -bdiag_out(c_im)], axis=1).astype(BF16)
    pr, pi = [lb_re], [lb_im]
    for _ in range(SUBLANES - 1):
        pr, pi = pr + [pr[-1] * lb_re - pi[-1] * lb_im], pi + [pr[-1] * lb_im + pi[-1] * lb_re]
    pw_r = jnp.stack(pr, axis=0).reshape(SUBLANES, nb, S5_CH).transpose(1, 0, 2)
    pw_i = jnp.stack(pi, axis=0).reshape(SUBLANES, nb, S5_CH).transpose(1, 0, 2)
    rowi = jnp.arange(SUBLANES)[None, :, None]
    kinds = []
    for s in (1, 2, 4):
        kinds.append(jnp.where(rowi >= s, pw_r[:, s - 1:s, :], 0.0))
        kinds.append(jnp.where(rowi >= s, pw_i[:, s - 1:s, :], 0.0))
    kinds += [pw_r, pw_i]
    tab = jnp.stack(kinds, axis=1)
    return bb, cd, tab


def _s5(u, a_re, a_im, log_step, b_re, b_im, c_re, c_im, d_skip, rows=512):
    t = u.shape[0]
    rows = min(rows, t)
    nb = S5_GROUPS // S5_GB
    bb, cd, tab = _s5_params(a_re, a_im, log_step, b_re, b_im, c_re, c_im)
    return pl.pallas_call(
        _s5_kernel,
        grid=(nb, t // rows),
        in_specs=[pl.BlockSpec((rows, LANES), lambda b, i: (i, b)),
                  pl.BlockSpec((1, LANES, 2 * S5_CH), lambda b, i: (b, 0, 0)),
                  pl.BlockSpec((1, 2 * S5_CH, LANES), lambda b, i: (b, 0, 0)),
                  pl.BlockSpec((1, 8, SUBLANES, S5_CH), lambda b, i: (b, 0, 0, 0)),
                  pl.BlockSpec((1, LANES), lambda b, i: (0, b))],
        out_specs=pl.BlockSpec((rows, LANES), lambda b, i: (i, b)),
        out_shape=jax.ShapeDtypeStruct((t, S5_WIDTH), F32),
        scratch_shapes=[pltpu.VMEM((rows, 2 * S5_CH), F32), pltpu.VMEM((SUBLANES, S5_CH), F32)],
        compiler_params=_cparams("parallel", "arbitrary"),
        name="s5_ssm",
    )(u, bb, cd, tab, d_skip.reshape(1, S5_WIDTH))


def _router_kernel(h_ref, nw_ref, wr_ref, br_ref, x_ref, idx_ref, wt_ref):
    h = h_ref[...]
    ms = jnp.mean(h * h, axis=-1, keepdims=True)
    x = h * lax.rsqrt(ms + NORM_EPS) * nw_ref[...]
    x_ref[...] = x
    xh, xm = _split2(x)
    logits = (_dot(xh, wr_ref[0]) + (_dot(xm, wr_ref[0]) + _dot(xh, wr_ref[1]))) + br_ref[...]
    lane = lax.broadcasted_iota(jnp.int32, logits.shape, 1)
    big = jnp.int32(4 * LANES)
    ng, epg = N_EXPERT_GROUPS, EXPERTS_PER_GROUP
    lg = jnp.where(lane < ng, logits, NEG_BIG)
    mg = jnp.max(lg, axis=-1, keepdims=True)
    g_sel = jnp.min(jnp.where(lg == mg, lane, big), axis=-1, keepdims=True)
    pg_top = 1.0 / jnp.sum(jnp.exp(lg - mg), axis=-1, keepdims=True)
    lo = ng + g_sel * epg
    le = jnp.where(jnp.logical_and(lane >= lo, lane < lo + epg), logits, NEG_BIG)
    m1 = jnp.max(le, axis=-1, keepdims=True)
    i1 = jnp.min(jnp.where(le == m1, lane, big), axis=-1, keepdims=True)
    zsum = jnp.sum(jnp.exp(le - m1), axis=-1, keepdims=True)
    le2 = jnp.where(lane == i1, NEG_BIG, le)
    m2 = jnp.max(le2, axis=-1, keepdims=True)
    i2 = jnp.min(jnp.where(le2 == m2, lane, big), axis=-1, keepdims=True)
    p1 = 1.0 / zsum
    p2 = jnp.exp(m2 - m1) / zsum
    c1 = pg_top * p1 / (p1 + p2)
    c2 = pg_top * p2 / (p1 + p2)
    idx_ref[...] = jnp.where(lane == 0, i1 - ng, jnp.where(lane == 1, i2 - ng, 0))
    wt_ref[...] = jnp.where(lane == 0, c1, jnp.where(lane == 1, c2, 0.0))


def _router(h, norm_w, wg, bg, we, be, rows=256):
    t, d = h.shape
    rows = min(rows, t)
    ncol = N_EXPERT_GROUPS + N_EXPERTS
    wr = jnp.pad(jnp.concatenate([wg, we], axis=1), ((0, 0), (0, LANES - ncol)))
    wr2 = jnp.stack(_split2(wr))
    br = jnp.pad(jnp.concatenate([bg, be]), (0, LANES - ncol)).reshape(1, LANES)
    return pl.pallas_call(
        _router_kernel,
        grid=(t // rows,),
        in_specs=[pl.BlockSpec((rows, d), lambda i: (i, 0)),
                  pl.BlockSpec((1, d), lambda i: (0, 0)),
                  pl.BlockSpec((2, d, LANES), lambda i: (0, 0, 0)),
                  pl.BlockSpec((1, LANES), lambda i: (0, 0))],
        out_specs=[pl.BlockSpec((rows, d), lambda i: (i, 0)),
                   pl.BlockSpec((rows, LANES), lambda i: (i, 0)),
                   pl.BlockSpec((rows, LANES), lambda i: (i, 0))],
        out_shape=[jax.ShapeDtypeStruct((t, d), F32),
                   jax.ShapeDtypeStruct((t, LANES), jnp.int32),
                   jax.ShapeDtypeStruct((t, LANES), F32)],
        compiler_params=_cparams("parallel"),
        name="moe_router",
    )(h, norm_w.reshape(1, d), wr2, br)


def _row_copy(src_hbm, dst_ref, sem, src_row, dst_row):
    return pltpu.make_async_copy(src_hbm.at[pl.ds(src_row, 1)], dst_ref.at[pl.ds(dst_row, 1)], sem)


DMA_ISSUE_UNROLL = 8


def _gather_rows_kernel(tok_ref, nv_ref, x_hbm, o_ref, buf_ref, sem, *, rows):
    b = pl.program_id(0)
    nb = pl.num_programs(0)
    slot = b % 2
    nxt = jnp.minimum(b + 1, nb - 1)

    def issue(blk, to_slot):
        def body(i, _):
            _row_copy(x_hbm, buf_ref.at[to_slot], sem.at[to_slot], tok_ref[blk * rows + i], i).start()
            return 0
        lax.fori_loop(0, rows, body, 0, unroll=DMA_ISSUE_UNROLL)

    @pl.when(jnp.logical_and(b == 0, nv_ref[0] > 0))
    def _():
        issue(0, 0)

    @pl.when(jnp.logical_and(b + 1 < nb, nv_ref[nxt] > 0))
    def _():
        issue(b + 1, 1 - slot)

    @pl.when(nv_ref[b] > 0)
    def _():
        pltpu.make_async_copy(x_hbm.at[pl.ds(0, rows)], buf_ref.at[slot], sem.at[slot]).wait()
        o_ref[...] = buf_ref[slot].astype(o_ref.dtype)

    @pl.when(nv_ref[b] == 0)
    def _():
        o_ref[...] = jnp.zeros_like(o_ref)


def _gather_rows(x, tok, nvalid, rows):
    n = tok.shape[0]
    d = x.shape[1]
    return pl.pallas_call(
        functools.partial(_gather_rows_kernel, rows=rows),
        grid_spec=pltpu.PrefetchScalarGridSpec(
            num_scalar_prefetch=2,
            grid=(n // rows,),
            in_specs=[pl.BlockSpec(memory_space=pl.ANY)],
            out_specs=pl.BlockSpec((rows, d), lambda b, tok, nv: (b, 0)),
            scratch_shapes=[pltpu.VMEM((2, rows, d), x.dtype), pltpu.SemaphoreType.DMA((2,))],
        ),
        out_shape=jax.ShapeDtypeStruct((n, d), BF16),
        compiler_params=_cparams("arbitrary"),
        name="moe_gather",
    )(tok, nvalid, x)


def _run_on_valid_rows(nvalid, out_ref, compute):
    rows = out_ref.shape[0]
    half = rows // 2

    @pl.when(nvalid > half)
    def _():
        compute(rows)

    @pl.when(jnp.logical_and(nvalid > 0, nvalid <= half))
    def _():
        compute(half)
        out_ref[pl.ds(half, rows - half), :] = jnp.zeros((rows - half, out_ref.shape[1]), out_ref.dtype)

    @pl.when(nvalid == 0)
    def _():
        out_ref[...] = jnp.zeros_like(out_ref)


def _expert_up_kernel(be_ref, nv_ref, chg_ref, x_ref, w1_ref, w3_ref, h_ref, w1b_ref, w3b_ref):
    b = pl.program_id(1)

    @pl.when(chg_ref[b] == 1)
    def _():
        w1b_ref[...] = w1_ref[0, 0].astype(BF16)
        w3b_ref[...] = w3_ref[0, 0].astype(BF16)

    def compute(nrows):
        x = x_ref[pl.ds(0, nrows), :]
        h1 = _dot(x, w1b_ref[...])
        h3 = _dot(x, w3b_ref[...])
        h_ref[pl.ds(0, nrows), :] = (h1 * _sigmoid(h1) * h3).astype(h_ref.dtype)

    _run_on_valid_rows(nv_ref[b], h_ref, compute)


def _expert_down_kernel(be_ref, nv_ref, chg_ref, h_ref, w2_ref, o_ref, w2b_ref):
    b = pl.program_id(1)

    @pl.when(chg_ref[b] == 1)
    def _():
        w2b_ref[...] = w2_ref[0, 0].astype(BF16)

    def compute(nrows):
        o_ref[pl.ds(0, nrows), :] = _dot(h_ref[pl.ds(0, nrows), :], w2b_ref[...])

    _run_on_valid_rows(nv_ref[b], o_ref, compute)


def _experts(xg, block_e, nvalid, changed, w1, w3, w2, layer, ff_tile=512, out_tile=2048):
    n, d = xg.shape
    rows = MOE_ROWS
    ff = w1.shape[3]
    ff_tile = min(ff_tile, ff)
    out_tile = min(out_tile, d)
    nb = n // rows
    hbuf = pl.pallas_call(
        _expert_up_kernel,
        grid_spec=pltpu.PrefetchScalarGridSpec(
            num_scalar_prefetch=3,
            grid=(ff // ff_tile, nb),
            in_specs=[pl.BlockSpec((rows, d), lambda f, b, be, nv, ch: (b, 0)),
                      pl.BlockSpec((1, 1, d, ff_tile), lambda f, b, be, nv, ch: (layer, be[b], 0, f)),
                      pl.BlockSpec((1, 1, d, ff_tile), lambda f, b, be, nv, ch: (layer, be[b], 0, f))],
            out_specs=pl.BlockSpec((rows, ff_tile), lambda f, b, be, nv, ch: (b, f)),
            scratch_shapes=[pltpu.VMEM((d, ff_tile), BF16), pltpu.VMEM((d, ff_tile), BF16)],
        ),
        out_shape=jax.ShapeDtypeStruct((n, ff), BF16),
        compiler_params=_cparams("arbitrary", "arbitrary"),
        name="moe_expert_up",
    )(block_e, nvalid, changed, xg, w1, w3)
    return pl.pallas_call(
        _expert_down_kernel,
        grid_spec=pltpu.PrefetchScalarGridSpec(
            num_scalar_prefetch=3,
            grid=(d // out_tile, nb),
            in_specs=[pl.BlockSpec((rows, ff), lambda j, b, be, nv, ch: (b, 0)),
                      pl.BlockSpec((1, 1, ff, out_tile), lambda j, b, be, nv, ch: (layer, be[b], 0, j))],
            out_specs=pl.BlockSpec((rows, out_tile), lambda j, b, be, nv, ch: (b, j)),
            scratch_shapes=[pltpu.VMEM((ff, out_tile), BF16)],
        ),
        out_shape=jax.ShapeDtypeStruct((n, d), F32),
        compiler_params=_cparams("arbitrary", "arbitrary"),
        name="moe_expert_down",
    )(block_e, nvalid, changed, hbuf, w2)


def _combine_kernel(dest_ref, h_ref, wt_ref, yb_hbm, o_ref, buf_ref, sem, *, rows):
    b = pl.program_id(0)
    nb = pl.num_programs(0)
    slot = b % 2

    def issue(blk, to_slot):
        def body(i, _):
            for j in range(TOP_K):
                _row_copy(yb_hbm, buf_ref.at[to_slot], sem.at[to_slot],
                          dest_ref[(blk * rows + i) * TOP_K + j], j * rows + i).start()
            return 0
        lax.fori_loop(0, rows, body, 0, unroll=DMA_ISSUE_UNROLL // TOP_K)

    @pl.when(b == 0)
    def _():
        issue(0, 0)

    @pl.when(b + 1 < nb)
    def _():
        issue(b + 1, 1 - slot)

    pltpu.make_async_copy(yb_hbm.at[pl.ds(0, TOP_K * rows)], buf_ref.at[slot], sem.at[slot]).wait()
    wt = wt_ref[...]
    o_ref[...] = (h_ref[...] + wt[:, 0:1] * buf_ref[slot, pl.ds(0, rows), :]
                  + wt[:, 1:2] * buf_ref[slot, pl.ds(rows, rows), :])


def _combine(h, wt, yb, dest, rows=128):
    t, d = h.shape
    rows = min(rows, t)
    return pl.pallas_call(
        functools.partial(_combine_kernel, rows=rows),
        grid_spec=pltpu.PrefetchScalarGridSpec(
            num_scalar_prefetch=1,
            grid=(t // rows,),
            in_specs=[pl.BlockSpec((rows, d), lambda i, dr: (i, 0)),
                      pl.BlockSpec((rows, LANES), lambda i, dr: (i, 0)),
                      pl.BlockSpec(memory_space=pl.ANY)],
            out_specs=pl.BlockSpec((rows, d), lambda i, dr: (i, 0)),
            scratch_shapes=[pltpu.VMEM((2, TOP_K * rows, d), F32), pltpu.SemaphoreType.DMA((2,))],
        ),
        out_shape=jax.ShapeDtypeStruct((t, d), F32),
        compiler_params=_cparams("arbitrary"),
        name="moe_combine",
    )(dest, h, wt, yb)


def _routing_tables(expert_idx, rows):
    t = expert_idx.shape[0]
    a = t * TOP_K
    e_n = N_EXPERTS
    i32 = jnp.int32
    flat_e = expert_idx.reshape(-1)
    onehot = (flat_e[:, None] == jnp.arange(e_n, dtype=i32)[None, :]).astype(i32)
    counts = jnp.sum(onehot, axis=0)
    rank = jnp.take_along_axis(jnp.cumsum(onehot, axis=0), flat_e[:, None], axis=1)[:, 0] - 1
    padded = (counts + rows - 1) // rows * rows
    pad_end = jnp.cumsum(padded)
    pad_start = pad_end - padded
    dest = (pad_start[flat_e] + rank).astype(i32)
    n_blocks = a // rows + e_n
    blk_start = jnp.arange(n_blocks, dtype=i32) * rows
    block_e = jnp.minimum(jnp.sum((pad_end[None, :] <= blk_start[:, None]).astype(i32), axis=1), e_n - 1)
    nvalid = jnp.clip(counts[block_e] - (blk_start - pad_start[block_e]), 0, rows).astype(i32)
    changed = jnp.concatenate([jnp.ones((1,), i32), (block_e[1:] != block_e[:-1]).astype(i32)])
    slot_tok = jnp.zeros((n_blocks * rows,), i32).at[dest].set(jnp.arange(a, dtype=i32) // TOP_K)
    return slot_tok, dest, block_e.astype(i32), nvalid, changed


def _moe(h, norm_w, wg, bg, we, be, w1, w3, w2, layer):
    x, idx, wt = _router(h, norm_w, wg, bg, we, be)
    slot_tok, dest, block_e, nvalid, changed = _routing_tables(idx[:, :TOP_K], MOE_ROWS)
    xg = _gather_rows(x, slot_tok, nvalid, MOE_ROWS)
    yb = _experts(xg, block_e, nvalid, changed, w1, w3, w2, layer)
    return _combine(h, wt, yb, dest)


def _even_layer(h, norm_w, w_in, q_norm_w, k_norm_w, conv_w, conv_b, dt_bias, a_log, d_skip, ssd_norm_w, w_out):
    xn = _rmsnorm(h, norm_w, BF16)
    w_in_t = jnp.swapaxes(w_in, 0, 1)
    proj = _matmul([xn], w_in_t, F32, n=EVEN_MAIN, name="even_in_proj", w_transposed=True)
    w_dt_t = jnp.pad(w_in_t[EVEN_MAIN:], ((0, LANES - SSD_HEADS), (0, 0)))
    dt_pad = _matmul([xn], w_dt_t, F32, n=LANES, name="even_dt_proj", w_transposed=True)
    y_att = _dilated_attention(proj, q_norm_w, k_norm_w)
    y_ssd = _mamba2(proj, dt_pad, conv_w, conv_b, dt_bias, a_log, d_skip, ssd_norm_w)
    return _matmul([y_att, y_ssd], w_out, F32, n=w_out.shape[1], epilogue="residual", extra=(h,),
                   name="even_out_proj")


def _odd_layer(h, norm_w, w_in, mu, w0, w2, a0, a2, g2, k_k, k_a, r_k, lnx_w, lnx_b,
               s5_a_re, s5_a_im, s5_log_step, s5_b_re, s5_b_im, s5_c_re, s5_c_im, s5_d,
               glu_w, glu_b, w_out):
    xn = _rmsnorm(h, norm_w, BF16)
    w_in_t = jnp.swapaxes(w_in, 0, 1)
    pc = _matmul([xn], w_in_t, F32, n=RWKV_PC, name="odd_in_proj_rwkv", w_transposed=True)
    u = _matmul([xn], w_in_t[RWKV_IN:], F32, n=S5_WIDTH, name="odd_in_proj_s5", w_transposed=True)
    r, k, v, lw, kk, a, g = _rwkv_prep(pc, mu, w0, w2, a0, a2, g2, k_k, k_a)
    y_c = _rwkv_scan(r, k, v, lw, kk, a, g, r_k.reshape(-1), lnx_w, lnx_b)
    y_s = _s5(u, s5_a_re, s5_a_im, s5_log_step, s5_b_re, s5_b_im, s5_c_re, s5_c_im, s5_d.reshape(-1))
    y_d = _matmul([y_s.astype(BF16)], glu_w, BF16, n=S5_WIDTH, epilogue="glu",
                  extra=(y_s, glu_b.reshape(1, -1)), name="s5_glu")
    return _matmul([y_c, y_d], w_out, F32, n=w_out.shape[1], epilogue="residual", extra=(h,),
                   name="odd_out_proj")


def kernel(x, norm_mix_w, norm_ffn_w, ev_w_in, ev_q_norm_w, ev_k_norm_w, ev_conv_w, ev_conv_b, ev_dt_bias, ev_a_log, ev_d_skip, ev_ssd_norm_w, ev_w_out, od_w_in, od_mu, od_w0, od_w2, od_a0, od_a2, od_g2, od_k_k, od_k_a, od_r_k, od_lnx_w, od_lnx_b, od_s5_a_re, od_s5_a_im, od_s5_log_step, od_s5_b_re, od_s5_b_im, od_s5_c_re, od_s5_c_im, od_s5_d, od_glu_w, od_glu_b, od_w_out, moe_wg, moe_bg, moe_we, moe_be, moe_w1, moe_w3, moe_w2):
    b, s, d = x.shape
    assert b == 1
    h = x.reshape(s, d)
    depth = norm_mix_w.shape[0]
    for layer in range(depth):
        i = layer // 2
        if layer % 2 == 0:
            h = _even_layer(h, norm_mix_w[layer], ev_w_in[i], ev_q_norm_w[i], ev_k_norm_w[i], ev_conv_w[i],
                            ev_conv_b[i], ev_dt_bias[i], ev_a_log[i], ev_d_skip[i], ev_ssd_norm_w[i], ev_w_out[i])
        else:
            h = _odd_layer(h, norm_mix_w[layer], od_w_in[i], od_mu[i], od_w0[i], od_w2[i], od_a0[i], od_a2[i],
                           od_g2[i], od_k_k[i], od_k_a[i], od_r_k[i], od_lnx_w[i], od_lnx_b[i], od_s5_a_re[i],
                           od_s5_a_im[i], od_s5_log_step[i], od_s5_b_re[i], od_s5_b_im[i], od_s5_c_re[i],
                           od_s5_c_im[i], od_s5_d[i], od_glu_w[i], od_glu_b[i], od_w_out[i])
        h = _moe(h, norm_ffn_w[layer], moe_wg[layer], moe_bg[layer], moe_we[layer], moe_be[layer],
                 moe_w1, moe_w3, moe_w2, layer)
    return h.reshape(b, s, d)
```

```python
import functools
import math

import jax
import jax.numpy as jnp
from jax import lax
from jax.experimental import pallas as pl
from jax.experimental.pallas import tpu as pltpu

F32 = jnp.float32
BF16 = jnp.bfloat16

LANES = 128
SUBLANES = 8
VMEM_LIMIT_BYTES = 52 * 1024 * 1024

NORM_EPS = 1e-6
NEG_BIG = -1e30

ATT_HEADS = 16
ATT_HEAD_DIM = 128
ATT_WIDTH = ATT_HEADS * ATT_HEAD_DIM
DILATED_BRANCHES = ((128, 1), (512, 4), (2048, 16))
ATT_TILE = max(w for w, _ in DILATED_BRANCHES)
ATT_BLOCKS_PER_STAGE = 4
ROPE_THETA = 500000.0
ROPE_DIM = ATT_HEAD_DIM // 4
SSD_HEADS = 32
SSD_HEAD_DIM = 64
SSD_WIDTH = SSD_HEADS * SSD_HEAD_DIM
SSD_GROUPS = 8
SSD_STATE = 128
SSD_CONV = 4
SSD_CHUNK = 128
SSD_CONV_CH = SSD_WIDTH + 2 * SSD_GROUPS * SSD_STATE
EVEN_MAIN = 3 * ATT_WIDTH + SSD_WIDTH + SSD_CONV_CH
RWKV_HEADS = 32
RWKV_HEAD_DIM = 64
RWKV_WIDTH = RWKV_HEADS * RWKV_HEAD_DIM
DECAY_RANK = 96
ICL_RANK = 96
GATE_RANK = 256
RWKV_IN = 3 * RWKV_WIDTH + DECAY_RANK + ICL_RANK + GATE_RANK
RWKV_LORA = 512
RWKV_PC = 3 * RWKV_WIDTH + RWKV_LORA
RWKV_GN_EPS = 64e-5
RWKV_CHUNK = 64
S5_GROUP = 16
S5_GROUPS = 128
S5_WIDTH = S5_GROUP * S5_GROUPS
S5_STATE = 64
S5_GB = 8
S5_CH = S5_GB * S5_STATE
N_EXPERT_GROUPS = 4
EXPERTS_PER_GROUP = 8
N_EXPERTS = N_EXPERT_GROUPS * EXPERTS_PER_GROUP
TOP_K = 2
MOE_ROWS = 256


def _cparams(*sem):
    return pltpu.CompilerParams(dimension_semantics=sem, vmem_limit_bytes=VMEM_LIMIT_BYTES)


def _split3(x):
    hi = x.astype(BF16)
    r1 = x - hi.astype(F32)
    mid = r1.astype(BF16)
    lo = (r1 - mid.astype(F32)).astype(BF16)
    return hi, mid, lo


def _split2(x):
    hi = x.astype(BF16)
    return hi, (x - hi.astype(F32)).astype(BF16)


def _dot(a, b):
    return jnp.dot(a, b, preferred_element_type=F32)


def _dot_nt(a, b):
    return lax.dot_general(a, b, (((1,), (1,)), ((), ())), preferred_element_type=F32)


def _dot_tn(a, b):
    return lax.dot_general(a, b, (((0,), (0,)), ((), ())), preferred_element_type=F32)


def _dot_exact_lhs(l_bf16, x):
    hi, mid, lo = _split3(x)
    return _dot(l_bf16, hi) + _dot(l_bf16, mid) + _dot(l_bf16, lo)


def _dot_exact_rhs(x, r_bf16):
    hi, mid, lo = _split3(x)
    return _dot(hi, r_bf16) + _dot(mid, r_bf16) + _dot(lo, r_bf16)


def _dot_hi_nt(a, b):
    a1, a2 = _split2(a)
    b1, b2 = _split2(b)
    return _dot_nt(a1, b1) + (_dot_nt(a1, b2) + _dot_nt(a2, b1))


def _sigmoid(x):
    return 1.0 / (1.0 + jnp.exp(-x))


def _softplus(x):
    return jnp.maximum(x, 0.0) + jnp.log(1.0 + jnp.exp(-jnp.abs(x)))


def _rmsnorm_kernel(x_ref, w_ref, o_ref):
    x = x_ref[...]
    ms = jnp.mean(x * x, axis=-1, keepdims=True)
    o_ref[...] = (x * lax.rsqrt(ms + NORM_EPS) * w_ref[...]).astype(o_ref.dtype)


def _rmsnorm(x, w, out_dtype, rows=256):
    t, d = x.shape
    rows = min(rows, t)
    return pl.pallas_call(
        _rmsnorm_kernel,
        grid=(t // rows,),
        in_specs=[pl.BlockSpec((rows, d), lambda i: (i, 0)),
                  pl.BlockSpec((1, d), lambda i: (0, 0))],
        out_specs=pl.BlockSpec((rows, d), lambda i: (i, 0)),
        out_shape=jax.ShapeDtypeStruct((t, d), out_dtype),
        compiler_params=_cparams("parallel"),
        name="rmsnorm",
    )(x, w.reshape(1, d))


def _mm_kernel(*refs, n_in, epilogue, w_transposed):
    a_refs = refs[:n_in]
    w_refs = refs[n_in:2 * n_in]
    wb_refs = refs[len(refs) - n_in:]
    rest = refs[2 * n_in:len(refs) - n_in]
    o_ref = rest[-1]
    dot = _dot_nt if w_transposed else _dot

    @pl.when(pl.program_id(1) == 0)
    def _():
        for w_ref, wb_ref in zip(w_refs, wb_refs):
            wb_ref[...] = w_ref[...].astype(BF16)

    acc = dot(a_refs[0][...], wb_refs[0][...])
    for a_ref, wb_ref in zip(a_refs[1:], wb_refs[1:]):
        acc = acc + dot(a_ref[...], wb_ref[...])
    if epilogue == "residual":
        acc = rest[0][...] + acc
    elif epilogue == "glu":
        acc = rest[0][...] * _sigmoid(acc + rest[1][...])
    o_ref[...] = acc.astype(o_ref.dtype)


def _matmul(a_list, w, out_dtype, *, n, tm=1024, tn=512, epilogue="none", extra=(), name="matmul",
            w_transposed=False):
    m = a_list[0].shape[0]
    tm = min(tm, m)
    tn = min(tn, n)
    assert m % tm == 0 and n % tn == 0, (m, n, tm, tn)
    kw = a_list[0].shape[1]
    assert all(a.shape[1] == kw for a in a_list) and w.shape[1 if w_transposed else 0] == kw * len(a_list)
    in_specs = [pl.BlockSpec((tm, kw), lambda j, i: (i, 0)) for _ in a_list]
    if w_transposed:
        w_block = (tn, kw)
        in_specs += [pl.BlockSpec(w_block, functools.partial(lambda j, i, r: (j, r), r=r)) for r in range(len(a_list))]
    else:
        w_block = (kw, tn)
        in_specs += [pl.BlockSpec(w_block, functools.partial(lambda j, i, r: (r, j), r=r)) for r in range(len(a_list))]
    if epilogue == "residual":
        in_specs.append(pl.BlockSpec((tm, tn), lambda j, i: (i, j)))
    elif epilogue == "glu":
        in_specs.append(pl.BlockSpec((tm, tn), lambda j, i: (i, j)))
        in_specs.append(pl.BlockSpec((1, tn), lambda j, i: (0, j)))
    return pl.pallas_call(
        functools.partial(_mm_kernel, n_in=len(a_list), epilogue=epilogue, w_transposed=w_transposed),
        grid=(n // tn, m // tm),
        in_specs=in_specs,
        out_specs=pl.BlockSpec((tm, tn), lambda j, i: (i, j)),
        out_shape=jax.ShapeDtypeStruct((m, n), out_dtype),
        scratch_shapes=[pltpu.VMEM(w_block, BF16) for _ in a_list],
        compiler_params=_cparams("parallel", "arbitrary"),
        name=name,
    )(*a_list, *([w] * len(a_list)), *extra)


def _norm_rope(x, w, cos, sin):
    ms = jnp.mean(x * x, axis=-1, keepdims=True)
    y = x * lax.rsqrt(ms + NORM_EPS) * w
    half = ROPE_DIM // 2
    lane = lax.broadcasted_iota(jnp.int32, y.shape, 1)
    partner = jnp.where(lane < half, pltpu.roll(y, LANES - half, 1), pltpu.roll(y, half, 1))
    return y * cos + partner * sin


def _rows(start, size, stride):
    return pl.ds(start, size, stride=stride) if stride > 1 else pl.ds(start, size)


def _attn_kernel(q_ref, kc_ref, kp_ref, vc_ref, vp_ref, cosc_ref, sinc_ref, cosp_ref, sinp_ref, qw_ref, kw_ref,
                 o_ref, qs_ref, ks_ref, ob0, ob1, ob2, lb0, lb1, lb2):
    i = pl.program_id(1)
    tile = q_ref.shape[0]
    w = LANES
    scale = ATT_HEAD_DIM ** -0.5
    qs_ref[...] = _norm_rope(q_ref[...], qw_ref[...], cosc_ref[...], sinc_ref[...]) * scale
    ks_ref[pl.ds(0, tile), :] = _norm_rope(kp_ref[...], kw_ref[...], cosp_ref[...], sinp_ref[...])
    ks_ref[pl.ds(tile, tile), :] = _norm_rope(kc_ref[...], kw_ref[...], cosc_ref[...], sinc_ref[...])
    qi = lax.broadcasted_iota(jnp.int32, (w, w), 0)
    kj = lax.broadcasted_iota(jnp.int32, (w, w), 1)
    keep_c = kj <= qi
    keep_p = kj >= qi
    keep_p0 = jnp.logical_and(keep_p, i > 0)
    blocks = []
    for (window, d), ob, lb in zip(DILATED_BRANCHES, (ob0, ob1, ob2), (lb0, lb1, lb2)):
        span = w * d
        assert span == window and tile % span == 0
        blocks += [(d, nn * span + j, nn == 0, ob, lb) for nn in range(tile // span) for j in range(d)]
    for g0 in range(0, len(blocks), ATT_BLOCKS_PER_STAGE):
        grp = blocks[g0:g0 + ATT_BLOCKS_PER_STAGE]
        q, kc, kp, vc, vp = [], [], [], [], []
        for d, off, first, _, _ in grp:
            span = w * d
            q.append(qs_ref[_rows(off, w, d), :].astype(BF16))
            kc.append(ks_ref[_rows(tile + off, w, d), :].astype(BF16))
            kp.append(ks_ref[_rows(tile + off - span, w, d), :].astype(BF16))
            vc.append(vc_ref[_rows(off, w, d), :].astype(BF16))
            if first:
                vp.append(vp_ref[_rows(tile + off - span, w, d), :].astype(BF16))
            else:
                vp.append(vc_ref[_rows(off - span, w, d), :].astype(BF16))
        s_c = [jnp.where(keep_c, _dot_nt(a, b), NEG_BIG) for a, b in zip(q, kc)]
        s_p = [jnp.where(keep_p0 if blk[2] else keep_p, _dot_nt(a, b), NEG_BIG) for blk, a, b in zip(grp, q, kp)]
        m = [jnp.maximum(jnp.max(a, axis=-1, keepdims=True), jnp.max(b, axis=-1, keepdims=True))
             for a, b in zip(s_c, s_p)]
        p_c = [jnp.exp(a - mm) for a, mm in zip(s_c, m)]
        p_p = [jnp.exp(a - mm) for a, mm in zip(s_p, m)]
        l = [jnp.sum(a, axis=-1, keepdims=True) + jnp.sum(b, axis=-1, keepdims=True) for a, b in zip(p_c, p_p)]
        o = [_dot(a.astype(BF16), va) + _dot(b.astype(BF16), vb) for a, b, va, vb in zip(p_c, p_p, vc, vp)]
        for (d, off, _, ob, lb), oo, ll, mm in zip(grp, o, l, m):
            ob[_rows(off, w, d), :] = oo / ll
            lb[_rows(off, w, d), :] = jnp.broadcast_to(mm + jnp.log(ll), oo.shape)
    a, b, c = lb0[...], lb1[...], lb2[...]
    m = jnp.maximum(jnp.maximum(a, b), c)
    wa, wb, wc = jnp.exp(a - m), jnp.exp(b - m), jnp.exp(c - m)
    o_ref[...] = ((wa * ob0[...] + wb * ob1[...] + wc * ob2[...]) / (wa + wb + wc)).astype(o_ref.dtype)


def _rope_tables(t):
    half = ROPE_DIM // 2
    inv_freq = ROPE_THETA ** (-jnp.arange(half, dtype=F32) / half)
    ang = jnp.arange(t, dtype=jnp.int32).astype(F32)[:, None] * inv_freq[None, :]
    cos, sin = jnp.cos(ang), jnp.sin(ang)
    ones = jnp.ones((t, ATT_HEAD_DIM - ROPE_DIM), F32)
    cos_f = jnp.concatenate([cos, cos, ones], axis=1)
    sin_f = jnp.concatenate([-sin, sin, 0.0 * ones], axis=1)
    return cos_f, sin_f


def _dilated_attention(proj, q_norm_w, k_norm_w):
    t = proj.shape[0]
    tile = ATT_TILE
    assert t % tile == 0
    cos_f, sin_f = _rope_tables(t)
    cur = lambda h, i: i
    prev = lambda h, i: jnp.maximum(i - 1, 0)
    blk = lambda rowf, col0: pl.BlockSpec((tile, LANES), lambda h, i: (rowf(h, i), col0 + h))
    tab = lambda rowf: pl.BlockSpec((tile, LANES), lambda h, i: (rowf(h, i), 0))
    const = pl.BlockSpec((1, LANES), lambda h, i: (0, 0))
    return pl.pallas_call(
        _attn_kernel,
        grid=(ATT_HEADS, t // tile),
        in_specs=[blk(cur, 0), blk(cur, ATT_HEADS), blk(prev, ATT_HEADS), blk(cur, 2 * ATT_HEADS),
                  blk(prev, 2 * ATT_HEADS), tab(cur), tab(cur), tab(prev), tab(prev), const, const],
        out_specs=pl.BlockSpec((tile, LANES), lambda h, i: (i, h)),
        out_shape=jax.ShapeDtypeStruct((t, ATT_WIDTH), BF16),
        scratch_shapes=[pltpu.VMEM((tile, LANES), F32), pltpu.VMEM((2 * tile, LANES), F32)]
        + [pltpu.VMEM((tile, LANES), F32)] * 6,
        compiler_params=_cparams("parallel", "arbitrary"),
        name="dilated_attention",
    )(proj, proj, proj, proj, proj, cos_f, sin_f, cos_f, sin_f,
      q_norm_w.reshape(1, LANES), k_norm_w.reshape(1, LANES))


def _ssd_kernel(z_ref, xbc_ref, xbcp_ref, dt_ref, dtt_ref, convw_ref, convb_ref, dtb_ref, dtbt_ref,
                alog_ref, alogt_ref, dskip_ref, normw_ref, o_ref, state_ref, y_ref):
    c = pl.program_id(0)
    q = SSD_CHUNK

    @pl.when(c == 0)
    def _():
        state_ref[...] = jnp.zeros_like(state_ref)

    cur = xbc_ref[...]
    prev = xbcp_ref[...] * jnp.where(c > 0, 1.0, 0.0)
    row = lax.broadcasted_iota(jnp.int32, cur.shape, 0)
    acc = convb_ref[...] + convw_ref[SSD_CONV - 1:SSD_CONV, :] * cur
    for k in range(1, SSD_CONV):
        sh = jnp.where(row < k, pltpu.roll(prev, k, 0), pltpu.roll(cur, k, 0))
        acc = acc + convw_ref[SSD_CONV - 1 - k:SSD_CONV - k, :] * sh
    xbc = acc * _sigmoid(acc)
    xs = xbc[:, :SSD_WIDTH]
    gn = SSD_GROUPS * SSD_STATE
    bm = xbc[:, SSD_WIDTH:SSD_WIDTH + gn].astype(BF16)
    cm = xbc[:, SSD_WIDTH + gn:].astype(BF16)

    li = lax.broadcasted_iota(jnp.int32, (q, q), 0)
    si = lax.broadcasted_iota(jnp.int32, (q, q), 1)
    causal = li >= si
    tri = jnp.where(causal, 1.0, 0.0).astype(BF16)
    tri_t = jnp.where(li <= si, 1.0, 0.0).astype(BF16)

    dt = _softplus(dt_ref[...] + dtb_ref[...])
    acs = _dot_exact_lhs(tri, dt * (-jnp.exp(alog_ref[...])))
    dt_t = _softplus(dtt_ref[...] + dtbt_ref[...])
    acs_t = _dot_exact_rhs(dt_t * (-jnp.exp(alogt_ref[...])), tri_t)

    epg = SSD_HEADS // SSD_GROUPS
    for g in range(SSD_GROUPS):
        bg = bm[:, g * SSD_STATE:(g + 1) * SSD_STATE]
        cg = cm[:, g * SSD_STATE:(g + 1) * SSD_STATE]
        cb = _dot_nt(cg, bg)
        hs = [g * epg + e for e in range(epg)]
        col = [acs[:, h:h + 1] for h in hs]
        last = [acs[q - 1:q, h:h + 1] for h in hs]
        decay = [jnp.exp(jnp.where(causal, c_ - acs_t[h:h + 1, :], NEG_BIG)) for h, c_ in zip(hs, col)]
        xdt = [xs[:, h * SSD_HEAD_DIM:(h + 1) * SSD_HEAD_DIM] * dt[:, h:h + 1] for h in hs]
        hstate = [state_ref[h] for h in hs]
        y = [_dot((cb * d_).astype(BF16), x_.astype(BF16)) for d_, x_ in zip(decay, xdt)]
        y = [y_ + _dot_nt(cg, s_.astype(BF16)) * jnp.exp(c_) for y_, s_, c_ in zip(y, hstate, col)]
        st = [_dot_tn((x_ * jnp.exp(l_ - c_)).astype(BF16), bg) for x_, l_, c_ in zip(xdt, last, col)]
        for h, s_, l_, t_, y_ in zip(hs, hstate, last, st, y):
            state_ref[h] = s_ * jnp.exp(l_) + t_
            y_ref[:, h * SSD_HEAD_DIM:(h + 1) * SSD_HEAD_DIM] = y_
    z = z_ref[...]
    y = (y_ref[...] + dskip_ref[...] * xs) * (z * _sigmoid(z))
    ms = jnp.mean(y * y, axis=-1, keepdims=True)
    o_ref[...] = (y * lax.rsqrt(ms + NORM_EPS) * normw_ref[...]).astype(o_ref.dtype)


def _mamba2(proj, dt_pad, conv_w, conv_b, dt_bias, a_log, d_skip, norm_w):
    t = proj.shape[0]
    q = SSD_CHUNK
    nc = t // q
    pad = LANES - SSD_HEADS
    dtb = jnp.pad(dt_bias, (0, pad)).reshape(1, LANES)
    alog = jnp.pad(a_log, (0, pad)).reshape(1, LANES)
    dskip = jnp.repeat(d_skip, SSD_HEAD_DIM).reshape(1, SSD_WIDTH)
    z_blk = (3 * ATT_WIDTH) // SSD_WIDTH
    xbc_blk = (3 * ATT_WIDTH + SSD_WIDTH) // SSD_CONV_CH
    assert z_blk * SSD_WIDTH == 3 * ATT_WIDTH and xbc_blk * SSD_CONV_CH == 3 * ATT_WIDTH + SSD_WIDTH
    const = lambda shape: pl.BlockSpec(shape, lambda c: (0,) * len(shape))
    return pl.pallas_call(
        _ssd_kernel,
        grid=(nc,),
        in_specs=[
            pl.BlockSpec((q, SSD_WIDTH), lambda c: (c, z_blk)),
            pl.BlockSpec((q, SSD_CONV_CH), lambda c: (c, xbc_blk)),
            pl.BlockSpec((q, SSD_CONV_CH), lambda c: (jnp.maximum(c - 1, 0), xbc_blk)),
            pl.BlockSpec((q, LANES), lambda c: (c, 0)),
            pl.BlockSpec((LANES, q), lambda c: (0, c)),
            const((SSD_CONV, SSD_CONV_CH)), const((1, SSD_CONV_CH)),
            const((1, LANES)), const((LANES, 1)), const((1, LANES)), const((LANES, 1)),
            const((1, SSD_WIDTH)), const((1, SSD_WIDTH)),
        ],
        out_specs=pl.BlockSpec((q, SSD_WIDTH), lambda c: (c, 0)),
        out_shape=jax.ShapeDtypeStruct((t, SSD_WIDTH), BF16),
        scratch_shapes=[pltpu.VMEM((SSD_HEADS, SSD_HEAD_DIM, SSD_STATE), F32),
                        pltpu.VMEM((q, SSD_WIDTH), F32)],
        compiler_params=_cparams("arbitrary"),
        name="mamba2_ssd",
    )(proj, proj, proj, dt_pad, dt_pad.T, conv_w, conv_b.reshape(1, -1), dtb, dtb.reshape(LANES, 1),
      alog, alog.reshape(LANES, 1), dskip, norm_w.reshape(1, -1))


def _rwkv_prep_kernel(pc_ref, pp_ref, mu_ref, w0_ref, w2_ref, a0_ref, a2_ref, g2_ref, kk_ref, ka_ref,
                      hsum_ref, hexp_ref, r_o, k_o, v_o, lw_o, kk_o, a_o, g_o):
    i = pl.program_id(0)
    cur = pc_ref[...]
    row = lax.broadcasted_iota(jnp.int32, cur.shape, 0)
    last_prev = pp_ref[SUBLANES - 1:SUBLANES, :] * jnp.where(i > 0, 1.0, 0.0)
    prev = jnp.where(row == 0, last_prev, pltpu.roll(cur, 1, 0))
    p = cur + (prev - cur) * mu_ref[...]
    w = RWKV_WIDTH
    r, k, v = p[:, :w], p[:, w:2 * w], p[:, 2 * w:3 * w]
    lora = p[:, 3 * w:]
    w_log = -_softplus(-(w0_ref[...] + _dot(jnp.tanh(lora).astype(BF16), w2_ref[...]))) - 0.5
    a = _sigmoid(a0_ref[...] + _dot(lora.astype(BF16), a2_ref[...]))
    g = _dot(_sigmoid(lora).astype(BF16), g2_ref[...])
    kkr = k * kk_ref[...]
    ss = _dot_exact_rhs(kkr * kkr, hsum_ref[...])
    nrm = jnp.maximum(jnp.sqrt(ss), 1e-12)
    inv = _dot_exact_rhs(1.0 / nrm, hexp_ref[...])
    r_o[...] = r
    k_o[...] = k * (1.0 + (a - 1.0) * ka_ref[...])
    v_o[...] = v
    lw_o[...] = -jnp.exp(w_log)
    kk_o[...] = kkr * inv
    a_o[...] = a
    g_o[...] = g


def _place_rows(m, start, total):
    return jnp.pad(m, ((start, total - start - m.shape[0]), (0, 0))).astype(BF16)


def _rwkv_prep(pc, mu, w0, w2, a0, a2, g2, k_k, k_a, rows=128):
    t = pc.shape[0]
    rows = min(rows, t)
    w = RWKV_WIDTH
    mu_p = jnp.pad(mu, (0, RWKV_PC - RWKV_IN)).reshape(1, RWKV_PC)
    w2_p = _place_rows(w2, 0, RWKV_LORA)
    a2_p = _place_rows(a2, DECAY_RANK, RWKV_LORA)
    g2_p = _place_rows(g2, DECAY_RANK + ICL_RANK, RWKV_LORA)
    head_of = jnp.arange(w) // RWKV_HEAD_DIM
    hsum = (head_of[:, None] == jnp.arange(LANES)[None, :]).astype(BF16)
    hexp = hsum.T
    const = lambda shape: pl.BlockSpec(shape, lambda i: (0,) * len(shape))
    rb = rows // SUBLANES
    return pl.pallas_call(
        _rwkv_prep_kernel,
        grid=(t // rows,),
        in_specs=[pl.BlockSpec((rows, RWKV_PC), lambda i: (i, 0)),
                  pl.BlockSpec((SUBLANES, RWKV_PC), lambda i: (jnp.maximum(i * rb - 1, 0), 0)),
                  const((1, RWKV_PC)), const((1, w)), const((RWKV_LORA, w)), const((1, w)),
                  const((RWKV_LORA, w)), const((RWKV_LORA, w)), const((1, w)), const((1, w)),
                  const((w, LANES)), const((LANES, w))],
        out_specs=[pl.BlockSpec((rows, w), lambda i: (i, 0))] * 7,
        out_shape=[jax.ShapeDtypeStruct((t, w), F32)] * 7,
        compiler_params=_cparams("parallel"),
        name="rwkv_prep",
    )(pc, pc, mu_p, w0.reshape(1, w), w2_p, a0.reshape(1, w), a2_p, g2_p, k_k.reshape(1, w),
      k_a.reshape(1, w), hsum, hexp)


def _rwkv_scan_kernel(r_ref, k_ref, v_ref, lw_ref, kk_ref, a_ref, g_ref, rk_ref, lnw_ref, lnb_ref,
                      o_ref, state_ref, *, heads_per_step):
    ci = pl.program_id(1)
    c = RWKV_CHUNK
    n = RWKV_HEAD_DIM

    @pl.when(ci == 0)
    def _():
        state_ref[...] = jnp.zeros_like(state_ref)

    ti = lax.broadcasted_iota(jnp.int32, (c, c), 0)
    tj = lax.broadcasted_iota(jnp.int32, (c, c), 1)
    tri = jnp.where(ti >= tj, 1.0, 0.0).astype(BF16)
    strict = ti > tj
    incl = ti >= tj
    eye = jnp.where(ti == tj, 1.0, 0.0)

    hs = range(heads_per_step)
    sls = [slice(hh * n, (hh + 1) * n) for hh in hs]
    cum = _dot_exact_lhs(tri, lw_ref[...])
    lw_all = lw_ref[...]
    w_incl_all = jnp.exp(cum)
    w_inv_all = jnp.exp(-cum)
    at_all = -kk_ref[...] * jnp.exp(cum - lw_all)
    bt_all = kk_ref[...] * a_ref[...] * w_inv_all
    rt_all = (r_ref[...] * w_incl_all).astype(BF16)
    kt_all = (k_ref[...] * w_inv_all).astype(BF16)
    v_all = v_ref[...].astype(BF16)
    at = [at_all[:, sl] for sl in sls]
    bt = [bt_all[:, sl] for sl in sls]
    rt = [rt_all[:, sl] for sl in sls]
    kt = [kt_all[:, sl] for sl in sls]
    vb = [v_all[:, sl] for sl in sls]
    atb = [x.astype(BF16) for x in at]
    btb = [x.astype(BF16) for x in bt]
    s0 = [state_ref[hh] for hh in hs]
    s0b = [x.astype(BF16) for x in s0]
    a_ab = [jnp.where(strict, _dot_hi_nt(at[h], bt[h]), 0.0) for h in hs]
    a_ak = [jnp.where(strict, _dot_nt(atb[h], kt[h]), 0.0).astype(BF16) for h in hs]
    a_rb = [jnp.where(incl, _dot_nt(rt[h], btb[h]), 0.0).astype(BF16) for h in hs]
    a_rk = [jnp.where(incl, _dot_nt(rt[h], kt[h]), 0.0).astype(BF16) for h in hs]
    rhs = [_dot_nt(atb[h], s0b[h]) + _dot(a_ak[h], vb[h]) for h in hs]
    tinv = [eye + x for x in a_ab]
    pw = [x.astype(BF16) for x in a_ab]
    for _ in range(int(math.log2(c)) - 1):
        pw = [_dot(x, x).astype(BF16) for x in pw]
        tinv = [t + _dot(t.astype(BF16), x) for t, x in zip(tinv, pw)]
    ub = [_dot(tinv[h].astype(BF16), rhs[h].astype(BF16)).astype(BF16) for h in hs]
    y = [_dot_nt(rt[h], s0b[h]) + _dot(a_rb[h], ub[h]) + _dot(a_rk[h], vb[h]) for h in hs]
    for h in hs:
        state_ref[h] = (s0[h] + _dot_tn(ub[h], btb[h]) + _dot_tn(vb[h], kt[h])) * w_incl_all[c - 1:c, sls[h]]
    rk_prod = r_ref[...] * k_ref[...] * rk_ref[...]
    for h in hs:
        sl = sls[h]
        mean = jnp.mean(y[h], axis=-1, keepdims=True)
        var = jnp.mean(jnp.square(y[h] - mean), axis=-1, keepdims=True)
        yn = (y[h] - mean) * lax.rsqrt(var + RWKV_GN_EPS) * lnw_ref[:, sl] + lnb_ref[:, sl]
        bonus = jnp.sum(rk_prod[:, sl], axis=-1, keepdims=True) * v_ref[:, sl]
        o_ref[:, sl] = ((yn + bonus) * g_ref[:, sl]).astype(o_ref.dtype)


def _rwkv_scan(r, k, v, lw, kk, a, g, r_k, lnx_w, lnx_b, heads_per_step=16):
    t, w = r.shape
    c = RWKV_CHUNK
    hb = heads_per_step
    wb = hb * RWKV_HEAD_DIM
    seq = pl.BlockSpec((c, wb), lambda h, i: (i, h))
    par = pl.BlockSpec((1, wb), lambda h, i: (0, h))
    return pl.pallas_call(
        functools.partial(_rwkv_scan_kernel, heads_per_step=hb),
        grid=(RWKV_HEADS // hb, t // c),
        in_specs=[seq] * 7 + [par] * 3,
        out_specs=seq,
        out_shape=jax.ShapeDtypeStruct((t, w), BF16),
        scratch_shapes=[pltpu.VMEM((hb, RWKV_HEAD_DIM, RWKV_HEAD_DIM), F32)],
        compiler_params=_cparams("parallel", "arbitrary"),
        name="rwkv_scan",
    )(r, k, v, lw, kk, a, g, r_k.reshape(1, w), lnx_w.reshape(1, w), lnx_b.reshape(1, w))


def _s5_kernel(u_ref, bb_ref, cd_ref, tab_ref, d_ref, o_ref, xs_ref, carry_ref):
    tt = pl.program_id(1)
    rows = u_ref.shape[0]
    nch = S5_CH

    @pl.when(tt == 0)
    def _():
        carry_ref[...] = jnp.zeros_like(carry_ref)

    u = u_ref[...]
    xs_ref[...] = _dot(u.astype(BF16), bb_ref[0])
    tab = tab_ref[0]
    steps = ((1, tab[0], tab[1]), (2, tab[2], tab[3]), (4, tab[4], tab[5]))
    pr, pi = tab[6], tab[7]

    def body(i, carry):
        cr, ci = carry
        base = pl.multiple_of(i * SUBLANES, SUBLANES)
        xr = xs_ref[pl.ds(base, SUBLANES), :nch]
        xi = xs_ref[pl.ds(base, SUBLANES), nch:]
        for s, mr, mi in steps:
            rr = pltpu.roll(xr, s, 0)
            ri = pltpu.roll(xi, s, 0)
            xr, xi = xr + mr * rr - mi * ri, xi + mr * ri + mi * rr
        xr, xi = xr + pr * cr - pi * ci, xi + pr * ci + pi * cr
        xs_ref[pl.ds(base, SUBLANES), :nch] = xr
        xs_ref[pl.ds(base, SUBLANES), nch:] = xi
        return xr[SUBLANES - 1:SUBLANES, :], xi[SUBLANES - 1:SUBLANES, :]

    cr, ci = lax.fori_loop(0, rows // SUBLANES, body, (carry_ref[0:1, :], carry_ref[1:2, :]))
    carry_ref[0:1, :] = cr
    carry_ref[1:2, :] = ci
    y = _dot(xs_ref[...].astype(BF16), cd_ref[0]) + d_ref[...] * u
    o_ref[...] = 0.5 * y * (1.0 + jnp.tanh(math.sqrt(2.0 / math.pi) * (y + 0.044715 * (y * y * y))))


def _s5_params(a_re, a_im, log_step, b_re, b_im, c_re, c_im):
    lam_re = jnp.minimum(a_re, -1e-4)
    lam_im = a_im
    step = jnp.exp(log_step)[:, None]
    mag = jnp.exp(lam_re * step)
    lb_re = mag * jnp.cos(lam_im * step)
    lb_im = mag * jnp.sin(lam_im * step)
    den = lam_re * lam_re + lam_im * lam_im
    f_re = ((lb_re - 1.0) * lam_re + lb_im * lam_im) / den
    f_im = (lb_im * lam_re - (lb_re - 1.0) * lam_im) / den
    bb_re = f_re[..., None] * b_re - f_im[..., None] * b_im
    bb_im = f_re[..., None] * b_im + f_im[..., None] * b_re
    nb = S5_GROUPS // S5_GB
    eye = jnp.eye(S5_GB, dtype=F32)

    def bdiag_in(m):
        mt = m.transpose(0, 2, 1).reshape(nb, S5_GB, S5_GROUP, S5_STATE)
        return jnp.einsum('bghp,gk->bghkp', mt, eye).reshape(nb, S5_GB * S5_GROUP, S5_CH)

    def bdiag_out(m):
        mt = m.transpose(0, 2, 1).reshape(nb, S5_GB, S5_STATE, S5_GROUP)
        return jnp.einsum('bgph,gk->bgpkh', mt, eye).reshape(nb, S5_CH, S5_GB * S5_GROUP)

    bb = jnp.concatenate([bdiag_in(bb_re), bdiag_in(bb_im)], axis=2).astype(BF16)
    cd = jnp.concatenate([bdiag_out(c_re), -bdiag_out(c_im)], axis=1).astype(BF16)
    pr, pi = [lb_re], [lb_im]
    for _ in range(SUBLANES - 1):
        pr, pi = pr + [pr[-1] * lb_re - pi[-1] * lb_im], pi + [pr[-1] * lb_im + pi[-1] * lb_re]
    pw_r = jnp.stack(pr, axis=0).reshape(SUBLANES, nb, S5_CH).transpose(1, 0, 2)
    pw_i = jnp.stack(pi, axis=0).reshape(SUBLANES, nb, S5_CH).transpose(1, 0, 2)
    rowi = jnp.arange(SUBLANES)[None, :, None]
    kinds = []
    for s in (1, 2, 4):
        kinds.append(jnp.where(rowi >= s, pw_r[:, s - 1:s, :], 0.0))
        kinds.append(jnp.where(rowi >= s, pw_i[:, s - 1:s, :], 0.0))
    kinds += [pw_r, pw_i]
    tab = jnp.stack(kinds, axis=1)
    return bb, cd, tab


def _s5(u, a_re, a_im, log_step, b_re, b_im, c_re, c_im, d_skip, rows=1024):
    t = u.shape[0]
    rows = min(rows, t)
    nb = S5_GROUPS // S5_GB
    bb, cd, tab = _s5_params(a_re, a_im, log_step, b_re, b_im, c_re, c_im)
    return pl.pallas_call(
        _s5_kernel,
        grid=(nb, t // rows),
        in_specs=[pl.BlockSpec((rows, LANES), lambda b, i: (i, b)),
                  pl.BlockSpec((1, LANES, 2 * S5_CH), lambda b, i: (b, 0, 0)),
                  pl.BlockSpec((1, 2 * S5_CH, LANES), lambda b, i: (b, 0, 0)),
                  pl.BlockSpec((1, 8, SUBLANES, S5_CH), lambda b, i: (b, 0, 0, 0)),
                  pl.BlockSpec((1, LANES), lambda b, i: (0, b))],
        out_specs=pl.BlockSpec((rows, LANES), lambda b, i: (i, b)),
        out_shape=jax.ShapeDtypeStruct((t, S5_WIDTH), F32),
        scratch_shapes=[pltpu.VMEM((rows, 2 * S5_CH), F32), pltpu.VMEM((SUBLANES, S5_CH), F32)],
        compiler_params=_cparams("parallel", "arbitrary"),
        name="s5_ssm",
    )(u, bb, cd, tab, d_skip.reshape(1, S5_WIDTH))


def _router_kernel(h_ref, nw_ref, wr_ref, br_ref, x_ref, idx_ref, wt_ref):
    h = h_ref[...]
    ms = jnp.mean(h * h, axis=-1, keepdims=True)
    x = h * lax.rsqrt(ms + NORM_EPS) * nw_ref[...]
    x_ref[...] = x
    xh, xm = _split2(x)
    logits = (_dot(xh, wr_ref[0]) + (_dot(xm, wr_ref[0]) + _dot(xh, wr_ref[1]))) + br_ref[...]
    lane = lax.broadcasted_iota(jnp.int32, logits.shape, 1)
    big = jnp.int32(4 * LANES)
    ng, epg = N_EXPERT_GROUPS, EXPERTS_PER_GROUP
    lg = jnp.where(lane < ng, logits, NEG_BIG)
    mg = jnp.max(lg, axis=-1, keepdims=True)
    g_sel = jnp.min(jnp.where(lg == mg, lane, big), axis=-1, keepdims=True)
    pg_top = 1.0 / jnp.sum(jnp.exp(lg - mg), axis=-1, keepdims=True)
    lo = ng + g_sel * epg
    le = jnp.where(jnp.logical_and(lane >= lo, lane < lo + epg), logits, NEG_BIG)
    m1 = jnp.max(le, axis=-1, keepdims=True)
    i1 = jnp.min(jnp.where(le == m1, lane, big), axis=-1, keepdims=True)
    zsum = jnp.sum(jnp.exp(le - m1), axis=-1, keepdims=True)
    le2 = jnp.where(lane == i1, NEG_BIG, le)
    m2 = jnp.max(le2, axis=-1, keepdims=True)
    i2 = jnp.min(jnp.where(le2 == m2, lane, big), axis=-1, keepdims=True)
    p1 = 1.0 / zsum
    p2 = jnp.exp(m2 - m1) / zsum
    c1 = pg_top * p1 / (p1 + p2)
    c2 = pg_top * p2 / (p1 + p2)
    idx_ref[...] = jnp.where(lane == 0, i1 - ng, jnp.where(lane == 1, i2 - ng, 0))
    wt_ref[...] = jnp.where(lane == 0, c1, jnp.where(lane == 1, c2, 0.0))


def _router(h, norm_w, wg, bg, we, be, rows=256):
    t, d = h.shape
    rows = min(rows, t)
    ncol = N_EXPERT_GROUPS + N_EXPERTS
    wr = jnp.pad(jnp.concatenate([wg, we], axis=1), ((0, 0), (0, LANES - ncol)))
    wr2 = jnp.stack(_split2(wr))
    br = jnp.pad(jnp.concatenate([bg, be]), (0, LANES - ncol)).reshape(1, LANES)
    return pl.pallas_call(
        _router_kernel,
        grid=(t // rows,),
        in_specs=[pl.BlockSpec((rows, d), lambda i: (i, 0)),
                  pl.BlockSpec((1, d), lambda i: (0, 0)),
                  pl.BlockSpec((2, d, LANES), lambda i: (0, 0, 0)),
                  pl.BlockSpec((1, LANES), lambda i: (0, 0))],
        out_specs=[pl.BlockSpec((rows, d), lambda i: (i, 0)),
                   pl.BlockSpec((rows, LANES), lambda i: (i, 0)),
                   pl.BlockSpec((rows, LANES), lambda i: (i, 0))],
        out_shape=[jax.ShapeDtypeStruct((t, d), F32),
                   jax.ShapeDtypeStruct((t, LANES), jnp.int32),
                   jax.ShapeDtypeStruct((t, LANES), F32)],
        compiler_params=_cparams("parallel"),
        name="moe_router",
    )(h, norm_w.reshape(1, d), wr2, br)


def _row_copy(src_hbm, dst_ref, sem, src_row, dst_row):
    return pltpu.make_async_copy(src_hbm.at[pl.ds(src_row, 1)], dst_ref.at[pl.ds(dst_row, 1)], sem)


DMA_ISSUE_UNROLL = 8


def _gather_rows_kernel(tok_ref, nv_ref, x_hbm, o_ref, buf_ref, sem, *, rows):
    b = pl.program_id(0)
    nb = pl.num_programs(0)
    slot = b % 2
    nxt = jnp.minimum(b + 1, nb - 1)

    def issue(blk, to_slot):
        def body(i, _):
            _row_copy(x_hbm, buf_ref.at[to_slot], sem.at[to_slot], tok_ref[blk * rows + i], i).start()
            return 0
        lax.fori_loop(0, rows, body, 0, unroll=DMA_ISSUE_UNROLL)

    @pl.when(jnp.logical_and(b == 0, nv_ref[0] > 0))
    def _():
        issue(0, 0)

    @pl.when(jnp.logical_and(b + 1 < nb, nv_ref[nxt] > 0))
    def _():
        issue(b + 1, 1 - slot)

    @pl.when(nv_ref[b] > 0)
    def _():
        pltpu.make_async_copy(x_hbm.at[pl.ds(0, rows)], buf_ref.at[slot], sem.at[slot]).wait()
        o_ref[...] = buf_ref[slot].astype(o_ref.dtype)

    @pl.when(nv_ref[b] == 0)
    def _():
        o_ref[...] = jnp.zeros_like(o_ref)


def _gather_rows(x, tok, nvalid, rows):
    n = tok.shape[0]
    d = x.shape[1]
    return pl.pallas_call(
        functools.partial(_gather_rows_kernel, rows=rows),
        grid_spec=pltpu.PrefetchScalarGridSpec(
            num_scalar_prefetch=2,
            grid=(n // rows,),
            in_specs=[pl.BlockSpec(memory_space=pl.ANY)],
            out_specs=pl.BlockSpec((rows, d), lambda b, tok, nv: (b, 0)),
            scratch_shapes=[pltpu.VMEM((2, rows, d), x.dtype), pltpu.SemaphoreType.DMA((2,))],
        ),
        out_shape=jax.ShapeDtypeStruct((n, d), BF16),
        compiler_params=_cparams("arbitrary"),
        name="moe_gather",
    )(tok, nvalid, x)


def _run_on_valid_rows(nvalid, out_ref, compute):
    rows = out_ref.shape[0]
    half = rows // 2

    @pl.when(nvalid > half)
    def _():
        compute(rows)

    @pl.when(jnp.logical_and(nvalid > 0, nvalid <= half))
    def _():
        compute(half)
        out_ref[pl.ds(half, rows - half), :] = jnp.zeros((rows - half, out_ref.shape[1]), out_ref.dtype)

    @pl.when(nvalid == 0)
    def _():
        out_ref[...] = jnp.zeros_like(out_ref)


def _expert_up_kernel(be_ref, nv_ref, chg_ref, x_ref, w1_ref, w3_ref, h_ref, w1b_ref, w3b_ref):
    b = pl.program_id(1)

    @pl.when(chg_ref[b] == 1)
    def _():
        w1b_ref[...] = w1_ref[0, 0].astype(BF16)
        w3b_ref[...] = w3_ref[0, 0].astype(BF16)

    def compute(nrows):
        x = x_ref[pl.ds(0, nrows), :]
        h1 = _dot(x, w1b_ref[...])
        h3 = _dot(x, w3b_ref[...])
        h_ref[pl.ds(0, nrows), :] = (h1 * _sigmoid(h1) * h3).astype(h_ref.dtype)

    _run_on_valid_rows(nv_ref[b], h_ref, compute)


def _expert_down_kernel(be_ref, nv_ref, chg_ref, h_ref, w2_ref, o_ref, w2b_ref):
    b = pl.program_id(1)

    @pl.when(chg_ref[b] == 1)
    def _():
        w2b_ref[...] = w2_ref[0, 0].astype(BF16)

    def compute(nrows):
        o_ref[pl.ds(0, nrows), :] = _dot(h_ref[pl.ds(0, nrows), :], w2b_ref[...])

    _run_on_valid_rows(nv_ref[b], o_ref, compute)


def _experts(xg, block_e, nvalid, changed, w1, w3, w2, layer, ff_tile=512, out_tile=2048):
    n, d = xg.shape
    rows = MOE_ROWS
    ff = w1.shape[3]
    ff_tile = min(ff_tile, ff)
    out_tile = min(out_tile, d)
    nb = n // rows
    hbuf = pl.pallas_call(
        _expert_up_kernel,
        grid_spec=pltpu.PrefetchScalarGridSpec(
            num_scalar_prefetch=3,
            grid=(ff // ff_tile, nb),
            in_specs=[pl.BlockSpec((rows, d), lambda f, b, be, nv, ch: (b, 0)),
                      pl.BlockSpec((1, 1, d, ff_tile), lambda f, b, be, nv, ch: (layer, be[b], 0, f)),
                      pl.BlockSpec((1, 1, d, ff_tile), lambda f, b, be, nv, ch: (layer, be[b], 0, f))],
            out_specs=pl.BlockSpec((rows, ff_tile), lambda f, b, be, nv, ch: (b, f)),
            scratch_shapes=[pltpu.VMEM((d, ff_tile), BF16), pltpu.VMEM((d, ff_tile), BF16)],
        ),
        out_shape=jax.ShapeDtypeStruct((n, ff), BF16),
        compiler_params=_cparams("arbitrary", "arbitrary"),
        name="moe_expert_up",
    )(block_e, nvalid, changed, xg, w1, w3)
    return pl.pallas_call(
        _expert_down_kernel,
        grid_spec=pltpu.PrefetchScalarGridSpec(
            num_scalar_prefetch=3,
            grid=(d // out_tile, nb),
            in_specs=[pl.BlockSpec((rows, ff), lambda j, b, be, nv, ch: (b, 0)),
                      pl.BlockSpec((1, 1, ff, out_tile), lambda j, b, be, nv, ch: (layer, be[b], 0, j))],
            out_specs=pl.BlockSpec((rows, out_tile), lambda j, b, be, nv, ch: (b, j)),
            scratch_shapes=[pltpu.VMEM((ff, out_tile), BF16)],
        ),
        out_shape=jax.ShapeDtypeStruct((n, d), F32),
        compiler_params=_cparams("arbitrary", "arbitrary"),
        name="moe_expert_down",
    )(block_e, nvalid, changed, hbuf, w2)


def _combine_kernel(dest_ref, h_ref, wt_ref, yb_hbm, o_ref, buf_ref, sem, *, rows):
    b = pl.program_id(0)
    nb = pl.num_programs(0)
    slot = b % 2

    def issue(blk, to_slot):
        def body(i, _):
            for j in range(TOP_K):
                _row_copy(yb_hbm, buf_ref.at[to_slot], sem.at[to_slot],
                          dest_ref[(blk * rows + i) * TOP_K + j], j * rows + i).start()
            return 0
        lax.fori_loop(0, rows, body, 0, unroll=DMA_ISSUE_UNROLL // TOP_K)

    @pl.when(b == 0)
    def _():
        issue(0, 0)

    @pl.when(b + 1 < nb)
    def _():
        issue(b + 1, 1 - slot)

    pltpu.make_async_copy(yb_hbm.at[pl.ds(0, TOP_K * rows)], buf_ref.at[slot], sem.at[slot]).wait()
    wt = wt_ref[...]
    o_ref[...] = (h_ref[...] + wt[:, 0:1] * buf_ref[slot, pl.ds(0, rows), :]
                  + wt[:, 1:2] * buf_ref[slot, pl.ds(rows, rows), :])


def _combine(h, wt, yb, dest, rows=128):
    t, d = h.shape
    rows = min(rows, t)
    return pl.pallas_call(
        functools.partial(_combine_kernel, rows=rows),
        grid_spec=pltpu.PrefetchScalarGridSpec(
            num_scalar_prefetch=1,
            grid=(t // rows,),
            in_specs=[pl.BlockSpec((rows, d), lambda i, dr: (i, 0)),
                      pl.BlockSpec((rows, LANES), lambda i, dr: (i, 0)),
                      pl.BlockSpec(memory_space=pl.ANY)],
            out_specs=pl.BlockSpec((rows, d), lambda i, dr: (i, 0)),
            scratch_shapes=[pltpu.VMEM((2, TOP_K * rows, d), F32), pltpu.SemaphoreType.DMA((2,))],
        ),
        out_shape=jax.ShapeDtypeStruct((t, d), F32),
        compiler_params=_cparams("arbitrary"),
        name="moe_combine",
    )(dest, h, wt, yb)


def _routing_tables(expert_idx, rows):
    t = expert_idx.shape[0]
    a = t * TOP_K
    e_n = N_EXPERTS
    i32 = jnp.int32
    flat_e = expert_idx.reshape(-1)
    onehot = (flat_e[:, None] == jnp.arange(e_n, dtype=i32)[None, :]).astype(i32)
    counts = jnp.sum(onehot, axis=0)
    padded = (counts + rows - 1) // rows * rows
    pad_end = jnp.cumsum(padded)
    pad_start = pad_end - padded
    dest = jnp.sum(onehot * (jnp.cumsum(onehot, axis=0) - 1 + pad_start[None, :]), axis=1).astype(i32)
    n_blocks = a // rows + e_n
    blk_start = jnp.arange(n_blocks, dtype=i32) * rows
    block_e = jnp.minimum(jnp.sum((pad_end[None, :] <= blk_start[:, None]).astype(i32), axis=1), e_n - 1)
    blk_onehot = (block_e[:, None] == jnp.arange(e_n, dtype=i32)[None, :]).astype(i32)
    blk_counts = jnp.sum(blk_onehot * counts[None, :], axis=1)
    blk_pad_start = jnp.sum(blk_onehot * pad_start[None, :], axis=1)
    nvalid = jnp.clip(blk_counts - (blk_start - blk_pad_start), 0, rows).astype(i32)
    changed = jnp.concatenate([jnp.ones((1,), i32), (block_e[1:] != block_e[:-1]).astype(i32)])
    n_slots = n_blocks * rows
    slot_tok = (jnp.arange(n_slots, dtype=i32) % t).at[dest].set(jnp.arange(a, dtype=i32) // TOP_K)
    return slot_tok, dest, block_e.astype(i32), nvalid, changed


def _moe(h, norm_w, wg, bg, we, be, w1, w3, w2, layer):
    x, idx, wt = _router(h, norm_w, wg, bg, we, be)
    slot_tok, dest, block_e, nvalid, changed = _routing_tables(idx[:, :TOP_K], MOE_ROWS)
    xg = _gather_rows(x, slot_tok, nvalid, MOE_ROWS)
    yb = _experts(xg, block_e, nvalid, changed, w1, w3, w2, layer)
    return _combine(h, wt, yb, dest)


def _even_layer(h, norm_w, w_in, q_norm_w, k_norm_w, conv_w, conv_b, dt_bias, a_log, d_skip, ssd_norm_w, w_out):
    xn = _rmsnorm(h, norm_w, BF16)
    w_in_t = jnp.swapaxes(w_in, 0, 1)
    proj = _matmul([xn], w_in_t, F32, n=EVEN_MAIN, name="even_in_proj", w_transposed=True)
    w_dt_t = jnp.pad(w_in_t[EVEN_MAIN:], ((0, LANES - SSD_HEADS), (0, 0)))
    dt_pad = _matmul([xn], w_dt_t, F32, n=LANES, name="even_dt_proj", w_transposed=True)
    y_att = _dilated_attention(proj, q_norm_w, k_norm_w)
    y_ssd = _mamba2(proj, dt_pad, conv_w, conv_b, dt_bias, a_log, d_skip, ssd_norm_w)
    return _matmul([y_att, y_ssd], w_out, F32, n=w_out.shape[1], epilogue="residual", extra=(h,),
                   name="even_out_proj")


def _odd_layer(h, norm_w, w_in, mu, w0, w2, a0, a2, g2, k_k, k_a, r_k, lnx_w, lnx_b,
               s5_a_re, s5_a_im, s5_log_step, s5_b_re, s5_b_im, s5_c_re, s5_c_im, s5_d,
               glu_w, glu_b, w_out):
    xn = _rmsnorm(h, norm_w, BF16)
    w_in_t = jnp.swapaxes(w_in, 0, 1)
    pc = _matmul([xn], w_in_t, F32, n=RWKV_PC, name="odd_in_proj_rwkv", w_transposed=True)
    u = _matmul([xn], w_in_t[RWKV_IN:], F32, n=S5_WIDTH, name="odd_in_proj_s5", w_transposed=True)
    r, k, v, lw, kk, a, g = _rwkv_prep(pc, mu, w0, w2, a0, a2, g2, k_k, k_a)
    y_c = _rwkv_scan(r, k, v, lw, kk, a, g, r_k.reshape(-1), lnx_w, lnx_b)
    y_s = _s5(u, s5_a_re, s5_a_im, s5_log_step, s5_b_re, s5_b_im, s5_c_re, s5_c_im, s5_d.reshape(-1))
    y_d = _matmul([y_s.astype(BF16)], glu_w, BF16, n=S5_WIDTH, epilogue="glu",
                  extra=(y_s, glu_b.reshape(1, -1)), name="s5_glu")
    return _matmul([y_c, y_d], w_out, F32, n=w_out.shape[1], epilogue="residual", extra=(h,),
                   name="odd_out_proj")


def kernel(x, norm_mix_w, norm_ffn_w, ev_w_in, ev_q_norm_w, ev_k_norm_w, ev_conv_w, ev_conv_b, ev_dt_bias, ev_a_log, ev_d_skip, ev_ssd_norm_w, ev_w_out, od_w_in, od_mu, od_w0, od_w2, od_a0, od_a2, od_g2, od_k_k, od_k_a, od_r_k, od_lnx_w, od_lnx_b, od_s5_a_re, od_s5_a_im, od_s5_log_step, od_s5_b_re, od_s5_b_im, od_s5_c_re, od_s5_c_im, od_s5_d, od_glu_w, od_glu_b, od_w_out, moe_wg, moe_bg, moe_we, moe_be, moe_w1, moe_w3, moe_w2):
    b, s, d = x.shape
    assert b == 1
    h = x.reshape(s, d)
    depth = norm_mix_w.shape[0]
    for layer in range(depth):
        i = layer // 2
        if layer % 2 == 0:
            h = _even_layer(h, norm_mix_w[layer], ev_w_in[i], ev_q_norm_w[i], ev_k_norm_w[i], ev_conv_w[i],
                            ev_conv_b[i], ev_dt_bias[i], ev_a_log[i], ev_d_skip[i], ev_ssd_norm_w[i], ev_w_out[i])
        else:
            h = _odd_layer(h, norm_mix_w[layer], od_w_in[i], od_mu[i], od_w0[i], od_w2[i], od_a0[i], od_a2[i],
                           od_g2[i], od_k_k[i], od_k_a[i], od_r_k[i], od_lnx_w[i], od_lnx_b[i], od_s5_a_re[i],
                           od_s5_a_im[i], od_s5_log_step[i], od_s5_b_re[i], od_s5_b_im[i], od_s5_c_re[i],
                           od_s5_c_im[i], od_s5_d[i], od_glu_w[i], od_glu_b[i], od_w_out[i])
        h = _moe(h, norm_ffn_w[layer], moe_wg[layer], moe_bg[layer], moe_we[layer], moe_be[layer],
                 moe_w1, moe_w3, moe_w2, layer)
    return h.reshape(b, s, d)
```

```python
import functools
import math

import jax
import jax.numpy as jnp
from jax import lax
from jax.experimental import pallas as pl
from jax.experimental.pallas import tpu as pltpu

F32 = jnp.float32
BF16 = jnp.bfloat16

LANES = 128
SUBLANES = 8
VMEM_LIMIT_BYTES = 52 * 1024 * 1024

NORM_EPS = 1e-6
NEG_BIG = -1e30

ATT_HEADS = 16
ATT_HEAD_DIM = 128
ATT_WIDTH = ATT_HEADS * ATT_HEAD_DIM
DILATED_BRANCHES = ((128, 1), (512, 4), (2048, 16))
ATT_TILE = max(w for w, _ in DILATED_BRANCHES)
ATT_BLOCKS_PER_STAGE = 4
ROPE_THETA = 500000.0
ROPE_DIM = ATT_HEAD_DIM // 4
SSD_HEADS = 32
SSD_HEAD_DIM = 64
SSD_WIDTH = SSD_HEADS * SSD_HEAD_DIM
SSD_GROUPS = 8
SSD_STATE = 128
SSD_CONV = 4
SSD_CHUNK = 128
SSD_CONV_CH = SSD_WIDTH + 2 * SSD_GROUPS * SSD_STATE
EVEN_MAIN = 3 * ATT_WIDTH + SSD_WIDTH + SSD_CONV_CH
RWKV_HEADS = 32
RWKV_HEAD_DIM = 64
RWKV_WIDTH = RWKV_HEADS * RWKV_HEAD_DIM
DECAY_RANK = 96
ICL_RANK = 96
GATE_RANK = 256
RWKV_IN = 3 * RWKV_WIDTH + DECAY_RANK + ICL_RANK + GATE_RANK
RWKV_LORA = 512
RWKV_PC = 3 * RWKV_WIDTH + RWKV_LORA
RWKV_GN_EPS = 64e-5
RWKV_CHUNK = 64
S5_GROUP = 16
S5_GROUPS = 128
S5_WIDTH = S5_GROUP * S5_GROUPS
S5_STATE = 64
S5_GB = 8
S5_CH = S5_GB * S5_STATE
N_EXPERT_GROUPS = 4
EXPERTS_PER_GROUP = 8
N_EXPERTS = N_EXPERT_GROUPS * EXPERTS_PER_GROUP
TOP_K = 2
MOE_ROWS = 256


def _cparams(*sem):
    return pltpu.CompilerParams(dimension_semantics=sem, vmem_limit_bytes=VMEM_LIMIT_BYTES)


def _split3(x):
    hi = x.astype(BF16)
    r1 = x - hi.astype(F32)
    mid = r1.astype(BF16)
    lo = (r1 - mid.astype(F32)).astype(BF16)
    return hi, mid, lo


def _split2(x):
    hi = x.astype(BF16)
    return hi, (x - hi.astype(F32)).astype(BF16)


def _dot(a, b):
    return jnp.dot(a, b, preferred_element_type=F32)


def _dot_nt(a, b):
    return lax.dot_general(a, b, (((1,), (1,)), ((), ())), preferred_element_type=F32)


def _dot_tn(a, b):
    return lax.dot_general(a, b, (((0,), (0,)), ((), ())), preferred_element_type=F32)


def _dot_exact_lhs(l_bf16, x):
    hi, mid, lo = _split3(x)
    return _dot(l_bf16, hi) + _dot(l_bf16, mid) + _dot(l_bf16, lo)


def _dot_exact_rhs(x, r_bf16):
    hi, mid, lo = _split3(x)
    return _dot(hi, r_bf16) + _dot(mid, r_bf16) + _dot(lo, r_bf16)


def _sigmoid(x):
    return 1.0 / (1.0 + jnp.exp(-x))


def _softplus(x):
    return jnp.maximum(x, 0.0) + jnp.log(1.0 + jnp.exp(-jnp.abs(x)))


def _rmsnorm_kernel(x_ref, w_ref, o_ref):
    x = x_ref[...]
    ms = jnp.mean(x * x, axis=-1, keepdims=True)
    o_ref[...] = (x * lax.rsqrt(ms + NORM_EPS) * w_ref[...]).astype(o_ref.dtype)


def _rmsnorm(x, w, out_dtype, rows=256):
    t, d = x.shape
    rows = min(rows, t)
    return pl.pallas_call(
        _rmsnorm_kernel,
        grid=(t // rows,),
        in_specs=[pl.BlockSpec((rows, d), lambda i: (i, 0)),
                  pl.BlockSpec((1, d), lambda i: (0, 0))],
        out_specs=pl.BlockSpec((rows, d), lambda i: (i, 0)),
        out_shape=jax.ShapeDtypeStruct((t, d), out_dtype),
        compiler_params=_cparams("parallel"),
        name="rmsnorm",
    )(x, w.reshape(1, d))


def _mm_kernel(*refs, n_in, epilogue, w_transposed):
    a_refs = refs[:n_in]
    w_refs = refs[n_in:2 * n_in]
    wb_refs = refs[len(refs) - n_in:]
    rest = refs[2 * n_in:len(refs) - n_in]
    o_ref = rest[-1]
    dot = _dot_nt if w_transposed else _dot

    @pl.when(pl.program_id(1) == 0)
    def _():
        for w_ref, wb_ref in zip(w_refs, wb_refs):
            wb_ref[...] = w_ref[...].astype(BF16)

    acc = dot(a_refs[0][...], wb_refs[0][...])
    for a_ref, wb_ref in zip(a_refs[1:], wb_refs[1:]):
        acc = acc + dot(a_ref[...], wb_ref[...])
    if epilogue == "residual":
        acc = rest[0][...] + acc
    elif epilogue == "glu":
        acc = rest[0][...] * _sigmoid(acc + rest[1][...])
    o_ref[...] = acc.astype(o_ref.dtype)


def _matmul(a_list, w, out_dtype, *, n, tm=1024, tn=512, epilogue="none", extra=(), name="matmul",
            w_transposed=False):
    m = a_list[0].shape[0]
    tm = min(tm, m)
    tn = min(tn, n)
    assert m % tm == 0 and n % tn == 0, (m, n, tm, tn)
    kw = a_list[0].shape[1]
    assert all(a.shape[1] == kw for a in a_list) and w.shape[1 if w_transposed else 0] == kw * len(a_list)
    in_specs = [pl.BlockSpec((tm, kw), lambda j, i: (i, 0)) for _ in a_list]
    if w_transposed:
        w_block = (tn, kw)
        in_specs += [pl.BlockSpec(w_block, functools.partial(lambda j, i, r: (j, r), r=r)) for r in range(len(a_list))]
    else:
        w_block = (kw, tn)
        in_specs += [pl.BlockSpec(w_block, functools.partial(lambda j, i, r: (r, j), r=r)) for r in range(len(a_list))]
    if epilogue == "residual":
        in_specs.append(pl.BlockSpec((tm, tn), lambda j, i: (i, j)))
    elif epilogue == "glu":
        in_specs.append(pl.BlockSpec((tm, tn), lambda j, i: (i, j)))
        in_specs.append(pl.BlockSpec((1, tn), lambda j, i: (0, j)))
    return pl.pallas_call(
        functools.partial(_mm_kernel, n_in=len(a_list), epilogue=epilogue, w_transposed=w_transposed),
        grid=(n // tn, m // tm),
        in_specs=in_specs,
        out_specs=pl.BlockSpec((tm, tn), lambda j, i: (i, j)),
        out_shape=jax.ShapeDtypeStruct((m, n), out_dtype),
        scratch_shapes=[pltpu.VMEM(w_block, BF16) for _ in a_list],
        compiler_params=_cparams("parallel", "arbitrary"),
        name=name,
    )(*a_list, *([w] * len(a_list)), *extra)


def _norm_rope(x, w, cos, sin):
    ms = jnp.mean(x * x, axis=-1, keepdims=True)
    y = x * lax.rsqrt(ms + NORM_EPS) * w
    half = ROPE_DIM // 2
    lane = lax.broadcasted_iota(jnp.int32, y.shape, 1)
    partner = jnp.where(lane < half, pltpu.roll(y, LANES - half, 1), pltpu.roll(y, half, 1))
    return y * cos + partner * sin


def _rows(start, size, stride):
    return pl.ds(start, size, stride=stride) if stride > 1 else pl.ds(start, size)


def _attn_kernel(q_ref, kc_ref, kp_ref, vc_ref, vp_ref, cosc_ref, sinc_ref, cosp_ref, sinp_ref, qw_ref, kw_ref,
                 o_ref, qs_ref, ks_ref, ob0, ob1, ob2, lb0, lb1, lb2):
    i = pl.program_id(1)
    tile = q_ref.shape[0]
    w = LANES
    scale = ATT_HEAD_DIM ** -0.5
    qs_ref[...] = _norm_rope(q_ref[...], qw_ref[...], cosc_ref[...], sinc_ref[...]) * scale
    ks_ref[pl.ds(0, tile), :] = _norm_rope(kp_ref[...], kw_ref[...], cosp_ref[...], sinp_ref[...])
    ks_ref[pl.ds(tile, tile), :] = _norm_rope(kc_ref[...], kw_ref[...], cosc_ref[...], sinc_ref[...])
    qi = lax.broadcasted_iota(jnp.int32, (w, w), 0)
    kj = lax.broadcasted_iota(jnp.int32, (w, w), 1)
    keep_c = kj <= qi
    keep_p = kj >= qi
    keep_p0 = jnp.logical_and(keep_p, i > 0)
    blocks = []
    for (window, d), ob, lb in zip(DILATED_BRANCHES, (ob0, ob1, ob2), (lb0, lb1, lb2)):
        span = w * d
        assert span == window and tile % span == 0
        blocks += [(d, nn * span + j, nn == 0, ob, lb) for nn in range(tile // span) for j in range(d)]
    for g0 in range(0, len(blocks), ATT_BLOCKS_PER_STAGE):
        grp = blocks[g0:g0 + ATT_BLOCKS_PER_STAGE]
        q, kc, kp, vc, vp = [], [], [], [], []
        for d, off, first, _, _ in grp:
            span = w * d
            q.append(qs_ref[_rows(off, w, d), :].astype(BF16))
            kc.append(ks_ref[_rows(tile + off, w, d), :].astype(BF16))
            kp.append(ks_ref[_rows(tile + off - span, w, d), :].astype(BF16))
            vc.append(vc_ref[_rows(off, w, d), :].astype(BF16))
            if first:
                vp.append(vp_ref[_rows(tile + off - span, w, d), :].astype(BF16))
            else:
                vp.append(vc_ref[_rows(off - span, w, d), :].astype(BF16))
        s_c = [jnp.where(keep_c, _dot_nt(a, b), NEG_BIG) for a, b in zip(q, kc)]
        s_p = [jnp.where(keep_p0 if blk[2] else keep_p, _dot_nt(a, b), NEG_BIG) for blk, a, b in zip(grp, q, kp)]
        m = [jnp.maximum(jnp.max(a, axis=-1, keepdims=True), jnp.max(b, axis=-1, keepdims=True))
             for a, b in zip(s_c, s_p)]
        p_c = [jnp.exp(a - mm) for a, mm in zip(s_c, m)]
        p_p = [jnp.exp(a - mm) for a, mm in zip(s_p, m)]
        l = [jnp.sum(a, axis=-1, keepdims=True) + jnp.sum(b, axis=-1, keepdims=True) for a, b in zip(p_c, p_p)]
        o = [_dot(a.astype(BF16), va) + _dot(b.astype(BF16), vb) for a, b, va, vb in zip(p_c, p_p, vc, vp)]
        for (d, off, _, ob, lb), oo, ll, mm in zip(grp, o, l, m):
            ob[_rows(off, w, d), :] = oo / ll
            lb[_rows(off, w, d), :] = jnp.broadcast_to(mm + jnp.log(ll), oo.shape)
    a, b, c = lb0[...], lb1[...], lb2[...]
    m = jnp.maximum(jnp.maximum(a, b), c)
    wa, wb, wc = jnp.exp(a - m), jnp.exp(b - m), jnp.exp(c - m)
    o_ref[...] = ((wa * ob0[...] + wb * ob1[...] + wc * ob2[...]) / (wa + wb + wc)).astype(o_ref.dtype)


def _rope_tables(t):
    half = ROPE_DIM // 2
    inv_freq = ROPE_THETA ** (-jnp.arange(half, dtype=F32) / half)
    ang = jnp.arange(t, dtype=jnp.int32).astype(F32)[:, None] * inv_freq[None, :]
    cos, sin = jnp.cos(ang), jnp.sin(ang)
    ones = jnp.ones((t, ATT_HEAD_DIM - ROPE_DIM), F32)
    cos_f = jnp.concatenate([cos, cos, ones], axis=1)
    sin_f = jnp.concatenate([-sin, sin, 0.0 * ones], axis=1)
    return cos_f, sin_f


def _dilated_attention(proj, q_norm_w, k_norm_w):
    t = proj.shape[0]
    tile = ATT_TILE
    assert t % tile == 0
    cos_f, sin_f = _rope_tables(t)
    cur = lambda h, i: i
    prev = lambda h, i: jnp.maximum(i - 1, 0)
    blk = lambda rowf, col0: pl.BlockSpec((tile, LANES), lambda h, i: (rowf(h, i), col0 + h))
    tab = lambda rowf: pl.BlockSpec((tile, LANES), lambda h, i: (rowf(h, i), 0))
    const = pl.BlockSpec((1, LANES), lambda h, i: (0, 0))
    return pl.pallas_call(
        _attn_kernel,
        grid=(ATT_HEADS, t // tile),
        in_specs=[blk(cur, 0), blk(cur, ATT_HEADS), blk(prev, ATT_HEADS), blk(cur, 2 * ATT_HEADS),
                  blk(prev, 2 * ATT_HEADS), tab(cur), tab(cur), tab(prev), tab(prev), const, const],
        out_specs=pl.BlockSpec((tile, LANES), lambda h, i: (i, h)),
        out_shape=jax.ShapeDtypeStruct((t, ATT_WIDTH), BF16),
        scratch_shapes=[pltpu.VMEM((tile, LANES), F32), pltpu.VMEM((2 * tile, LANES), F32)]
        + [pltpu.VMEM((tile, LANES), F32)] * 6,
        compiler_params=_cparams("parallel", "arbitrary"),
        name="dilated_attention",
    )(proj, proj, proj, proj, proj, cos_f, sin_f, cos_f, sin_f,
      q_norm_w.reshape(1, LANES), k_norm_w.reshape(1, LANES))


def _ssd_kernel(z_ref, xbc_ref, xbcp_ref, dt_ref, dtt_ref, convw_ref, convb_ref, dtb_ref, dtbt_ref,
                alog_ref, alogt_ref, dskip_ref, normw_ref, o_ref, state_ref, y_ref):
    c = pl.program_id(0)
    q = SSD_CHUNK

    @pl.when(c == 0)
    def _():
        state_ref[...] = jnp.zeros_like(state_ref)

    cur = xbc_ref[...]
    prev = xbcp_ref[...] * jnp.where(c > 0, 1.0, 0.0)
    row = lax.broadcasted_iota(jnp.int32, cur.shape, 0)
    acc = convb_ref[...] + convw_ref[SSD_CONV - 1:SSD_CONV, :] * cur
    for k in range(1, SSD_CONV):
        sh = jnp.where(row < k, pltpu.roll(prev, k, 0), pltpu.roll(cur, k, 0))
        acc = acc + convw_ref[SSD_CONV - 1 - k:SSD_CONV - k, :] * sh
    xbc = acc * _sigmoid(acc)
    xs = xbc[:, :SSD_WIDTH]
    gn = SSD_GROUPS * SSD_STATE
    bm = xbc[:, SSD_WIDTH:SSD_WIDTH + gn].astype(BF16)
    cm = xbc[:, SSD_WIDTH + gn:].astype(BF16)

    li = lax.broadcasted_iota(jnp.int32, (q, q), 0)
    si = lax.broadcasted_iota(jnp.int32, (q, q), 1)
    causal = li >= si
    tri = jnp.where(causal, 1.0, 0.0).astype(BF16)
    tri_t = jnp.where(li <= si, 1.0, 0.0).astype(BF16)

    dt = _softplus(dt_ref[...] + dtb_ref[...])
    acs = _dot_exact_lhs(tri, dt * (-jnp.exp(alog_ref[...])))
    dt_t = _softplus(dtt_ref[...] + dtbt_ref[...])
    acs_t = _dot_exact_rhs(dt_t * (-jnp.exp(alogt_ref[...])), tri_t)

    epg = SSD_HEADS // SSD_GROUPS
    for g in range(SSD_GROUPS):
        bg = bm[:, g * SSD_STATE:(g + 1) * SSD_STATE]
        cg = cm[:, g * SSD_STATE:(g + 1) * SSD_STATE]
        cb = _dot_nt(cg, bg)
        hs = [g * epg + e for e in range(epg)]
        col = [acs[:, h:h + 1] for h in hs]
        last = [acs[q - 1:q, h:h + 1] for h in hs]
        decay = [jnp.exp(jnp.where(causal, c_ - acs_t[h:h + 1, :], NEG_BIG)) for h, c_ in zip(hs, col)]
        xdt = [xs[:, h * SSD_HEAD_DIM:(h + 1) * SSD_HEAD_DIM] * dt[:, h:h + 1] for h in hs]
        hstate = [state_ref[h] for h in hs]
        y = [_dot((cb * d_).astype(BF16), x_.astype(BF16)) for d_, x_ in zip(decay, xdt)]
        y = [y_ + _dot_nt(cg, s_.astype(BF16)) * jnp.exp(c_) for y_, s_, c_ in zip(y, hstate, col)]
        st = [_dot_tn((x_ * jnp.exp(l_ - c_)).astype(BF16), bg) for x_, l_, c_ in zip(xdt, last, col)]
        for h, s_, l_, t_, y_ in zip(hs, hstate, last, st, y):
            state_ref[h] = s_ * jnp.exp(l_) + t_
            y_ref[:, h * SSD_HEAD_DIM:(h + 1) * SSD_HEAD_DIM] = y_
    z = z_ref[...]
    y = (y_ref[...] + dskip_ref[...] * xs) * (z * _sigmoid(z))
    ms = jnp.mean(y * y, axis=-1, keepdims=True)
    o_ref[...] = (y * lax.rsqrt(ms + NORM_EPS) * normw_ref[...]).astype(o_ref.dtype)


def _mamba2(proj, dt_pad, conv_w, conv_b, dt_bias, a_log, d_skip, norm_w):
    t = proj.shape[0]
    q = SSD_CHUNK
    nc = t // q
    pad = LANES - SSD_HEADS
    dtb = jnp.pad(dt_bias, (0, pad)).reshape(1, LANES)
    alog = jnp.pad(a_log, (0, pad)).reshape(1, LANES)
    dskip = jnp.repeat(d_skip, SSD_HEAD_DIM).reshape(1, SSD_WIDTH)
    z_blk = (3 * ATT_WIDTH) // SSD_WIDTH
    xbc_blk = (3 * ATT_WIDTH + SSD_WIDTH) // SSD_CONV_CH
    assert z_blk * SSD_WIDTH == 3 * ATT_WIDTH and xbc_blk * SSD_CONV_CH == 3 * ATT_WIDTH + SSD_WIDTH
    const = lambda shape: pl.BlockSpec(shape, lambda c: (0,) * len(shape))
    return pl.pallas_call(
        _ssd_kernel,
        grid=(nc,),
        in_specs=[
            pl.BlockSpec((q, SSD_WIDTH), lambda c: (c, z_blk)),
            pl.BlockSpec((q, SSD_CONV_CH), lambda c: (c, xbc_blk)),
            pl.BlockSpec((q, SSD_CONV_CH), lambda c: (jnp.maximum(c - 1, 0), xbc_blk)),
            pl.BlockSpec((q, LANES), lambda c: (c, 0)),
            pl.BlockSpec((LANES, q), lambda c: (0, c)),
            const((SSD_CONV, SSD_CONV_CH)), const((1, SSD_CONV_CH)),
            const((1, LANES)), const((LANES, 1)), const((1, LANES)), const((LANES, 1)),
            const((1, SSD_WIDTH)), const((1, SSD_WIDTH)),
        ],
        out_specs=pl.BlockSpec((q, SSD_WIDTH), lambda c: (c, 0)),
        out_shape=jax.ShapeDtypeStruct((t, SSD_WIDTH), BF16),
        scratch_shapes=[pltpu.VMEM((SSD_HEADS, SSD_HEAD_DIM, SSD_STATE), F32),
                        pltpu.VMEM((q, SSD_WIDTH), F32)],
        compiler_params=_cparams("arbitrary"),
        name="mamba2_ssd",
    )(proj, proj, proj, dt_pad, dt_pad.T, conv_w, conv_b.reshape(1, -1), dtb, dtb.reshape(LANES, 1),
      alog, alog.reshape(LANES, 1), dskip, norm_w.reshape(1, -1))


def _rwkv_prep_kernel(pc_ref, pp_ref, mu_ref, w0_ref, w2_ref, a0_ref, a2_ref, g2_ref, kk_ref, ka_ref,
                      hsum_ref, hexp_ref, r_o, k_o, v_o, lw_o, kk_o, a_o, g_o):
    i = pl.program_id(0)
    cur = pc_ref[...]
    row = lax.broadcasted_iota(jnp.int32, cur.shape, 0)
    last_prev = pp_ref[SUBLANES - 1:SUBLANES, :] * jnp.where(i > 0, 1.0, 0.0)
    prev = jnp.where(row == 0, last_prev, pltpu.roll(cur, 1, 0))
    p = cur + (prev - cur) * mu_ref[...]
    w = RWKV_WIDTH
    r, k, v = p[:, :w], p[:, w:2 * w], p[:, 2 * w:3 * w]
    lora = p[:, 3 * w:]
    w_log = -_softplus(-(w0_ref[...] + _dot(jnp.tanh(lora).astype(BF16), w2_ref[...]))) - 0.5
    a = _sigmoid(a0_ref[...] + _dot(lora.astype(BF16), a2_ref[...]))
    g = _dot(_sigmoid(lora).astype(BF16), g2_ref[...])
    kkr = k * kk_ref[...]
    ss = _dot_exact_rhs(kkr * kkr, hsum_ref[...])
    nrm = jnp.maximum(jnp.sqrt(ss), 1e-12)
    inv = _dot_exact_rhs(1.0 / nrm, hexp_ref[...])
    r_o[...] = r
    k_o[...] = k * (1.0 + (a - 1.0) * ka_ref[...])
    v_o[...] = v
    lw_o[...] = -jnp.exp(w_log)
    kk_o[...] = kkr * inv
    a_o[...] = a
    g_o[...] = g


def _place_rows(m, start, total):
    return jnp.pad(m, ((start, total - start - m.shape[0]), (0, 0))).astype(BF16)


def _rwkv_prep(pc, mu, w0, w2, a0, a2, g2, k_k, k_a, rows=128):
    t = pc.shape[0]
    rows = min(rows, t)
    w = RWKV_WIDTH
    mu_p = jnp.pad(mu, (0, RWKV_PC - RWKV_IN)).reshape(1, RWKV_PC)
    w2_p = _place_rows(w2, 0, RWKV_LORA)
    a2_p = _place_rows(a2, DECAY_RANK, RWKV_LORA)
    g2_p = _place_rows(g2, DECAY_RANK + ICL_RANK, RWKV_LORA)
    head_of = jnp.arange(w) // RWKV_HEAD_DIM
    hsum = (head_of[:, None] == jnp.arange(LANES)[None, :]).astype(BF16)
    hexp = hsum.T
    const = lambda shape: pl.BlockSpec(shape, lambda i: (0,) * len(shape))
    rb = rows // SUBLANES
    return pl.pallas_call(
        _rwkv_prep_kernel,
        grid=(t // rows,),
        in_specs=[pl.BlockSpec((rows, RWKV_PC), lambda i: (i, 0)),
                  pl.BlockSpec((SUBLANES, RWKV_PC), lambda i: (jnp.maximum(i * rb - 1, 0), 0)),
                  const((1, RWKV_PC)), const((1, w)), const((RWKV_LORA, w)), const((1, w)),
                  const((RWKV_LORA, w)), const((RWKV_LORA, w)), const((1, w)), const((1, w)),
                  const((w, LANES)), const((LANES, w))],
        out_specs=[pl.BlockSpec((rows, w), lambda i: (i, 0))] * 7,
        out_shape=[jax.ShapeDtypeStruct((t, w), F32)] * 7,
        compiler_params=_cparams("parallel"),
        name="rwkv_prep",
    )(pc, pc, mu_p, w0.reshape(1, w), w2_p, a0.reshape(1, w), a2_p, g2_p, k_k.reshape(1, w),
      k_a.reshape(1, w), hsum, hexp)


def _rwkv_scan_kernel(r_ref, k_ref, v_ref, lw_ref, kk_ref, a_ref, g_ref, rk_ref, lnw_ref, lnb_ref,
                      o_ref, state_ref, *, heads_per_step):
    ci = pl.program_id(1)
    c = RWKV_CHUNK
    n = RWKV_HEAD_DIM

    @pl.when(ci == 0)
    def _():
        state_ref[...] = jnp.zeros_like(state_ref)

    ti = lax.broadcasted_iota(jnp.int32, (c, c), 0)
    tj = lax.broadcasted_iota(jnp.int32, (c, c), 1)
    tri = jnp.where(ti >= tj, 1.0, 0.0).astype(BF16)
    strict = ti > tj
    incl = ti >= tj
    eye = jnp.where(ti == tj, 1.0, 0.0)

    hs = range(heads_per_step)
    sls = [slice(hh * n, (hh + 1) * n) for hh in hs]
    cum = _dot_exact_lhs(tri, lw_ref[...])
    lw_all = lw_ref[...]
    w_incl_all = jnp.exp(cum)
    w_inv_all = jnp.exp(-cum)
    at_all = -kk_ref[...] * jnp.exp(cum - lw_all)
    bt_all = kk_ref[...] * a_ref[...] * w_inv_all
    rt_all = (r_ref[...] * w_incl_all).astype(BF16)
    kt_all = (k_ref[...] * w_inv_all).astype(BF16)
    v_all = v_ref[...].astype(BF16)
    at = [at_all[:, sl] for sl in sls]
    bt = [bt_all[:, sl] for sl in sls]
    rt = [rt_all[:, sl] for sl in sls]
    kt = [kt_all[:, sl] for sl in sls]
    vb = [v_all[:, sl] for sl in sls]
    atb = [x.astype(BF16) for x in at]
    btb = [x.astype(BF16) for x in bt]
    s0 = [state_ref[hh] for hh in hs]
    s0b = [x.astype(BF16) for x in s0]
    a_ab = [jnp.where(strict, _dot_nt(atb[h], btb[h]), 0.0) for h in hs]
    a_ak = [jnp.where(strict, _dot_nt(atb[h], kt[h]), 0.0).astype(BF16) for h in hs]
    a_rb = [jnp.where(incl, _dot_nt(rt[h], btb[h]), 0.0).astype(BF16) for h in hs]
    a_rk = [jnp.where(incl, _dot_nt(rt[h], kt[h]), 0.0).astype(BF16) for h in hs]
    rhs = [_dot_nt(atb[h], s0b[h]) + _dot(a_ak[h], vb[h]) for h in hs]
    tinv = [eye + x for x in a_ab]
    pw = [x.astype(BF16) for x in a_ab]
    for _ in range(int(math.log2(c)) - 1):
        pw = [_dot(x, x).astype(BF16) for x in pw]
        tinv = [t + _dot(t.astype(BF16), x) for t, x in zip(tinv, pw)]
    ub = [_dot(tinv[h].astype(BF16), rhs[h].astype(BF16)).astype(BF16) for h in hs]
    y = [_dot_nt(rt[h], s0b[h]) + _dot(a_rb[h], ub[h]) + _dot(a_rk[h], vb[h]) for h in hs]
    for h in hs:
        state_ref[h] = (s0[h] + _dot_tn(ub[h], btb[h]) + _dot_tn(vb[h], kt[h])) * w_incl_all[c - 1:c, sls[h]]
    rk_prod = r_ref[...] * k_ref[...] * rk_ref[...]
    for h in hs:
        sl = sls[h]
        mean = jnp.mean(y[h], axis=-1, keepdims=True)
        var = jnp.mean(jnp.square(y[h] - mean), axis=-1, keepdims=True)
        yn = (y[h] - mean) * lax.rsqrt(var + RWKV_GN_EPS) * lnw_ref[:, sl] + lnb_ref[:, sl]
        bonus = jnp.sum(rk_prod[:, sl], axis=-1, keepdims=True) * v_ref[:, sl]
        o_ref[:, sl] = ((yn + bonus) * g_ref[:, sl]).astype(o_ref.dtype)


def _rwkv_scan(r, k, v, lw, kk, a, g, r_k, lnx_w, lnx_b, heads_per_step=32):
    t, w = r.shape
    c = RWKV_CHUNK
    hb = heads_per_step
    wb = hb * RWKV_HEAD_DIM
    seq = pl.BlockSpec((c, wb), lambda h, i: (i, h))
    par = pl.BlockSpec((1, wb), lambda h, i: (0, h))
    return pl.pallas_call(
        functools.partial(_rwkv_scan_kernel, heads_per_step=hb),
        grid=(RWKV_HEADS // hb, t // c),
        in_specs=[seq] * 7 + [par] * 3,
        out_specs=seq,
        out_shape=jax.ShapeDtypeStruct((t, w), BF16),
        scratch_shapes=[pltpu.VMEM((hb, RWKV_HEAD_DIM, RWKV_HEAD_DIM), F32)],
        compiler_params=_cparams("parallel", "arbitrary"),
        name="rwkv_scan",
    )(r, k, v, lw, kk, a, g, r_k.reshape(1, w), lnx_w.reshape(1, w), lnx_b.reshape(1, w))


def _s5_kernel(u_ref, bb_ref, cd_ref, tab_ref, d_ref, o_ref, xs_ref, carry_ref):
    tt = pl.program_id(1)
    rows = u_ref.shape[0]
    nch = S5_CH

    @pl.when(tt == 0)
    def _():
        carry_ref[...] = jnp.zeros_like(carry_ref)

    u = u_ref[...]
    xs_ref[...] = _dot(u.astype(BF16), bb_ref[0])
    tab = tab_ref[0]
    steps = ((1, tab[0], tab[1]), (2, tab[2], tab[3]), (4, tab[4], tab[5]))
    pr, pi = tab[6], tab[7]

    def body(i, carry):
        cr, ci = carry
        base = pl.multiple_of(i * SUBLANES, SUBLANES)
        xr = xs_ref[pl.ds(base, SUBLANES), :nch]
        xi = xs_ref[pl.ds(base, SUBLANES), nch:]
        for s, mr, mi in steps:
            rr = pltpu.roll(xr, s, 0)
            ri = pltpu.roll(xi, s, 0)
            xr, xi = xr + mr * rr - mi * ri, xi + mr * ri + mi * rr
        xr, xi = xr + pr * cr - pi * ci, xi + pr * ci + pi * cr
        xs_ref[pl.ds(base, SUBLANES), :nch] = xr
        xs_ref[pl.ds(base, SUBLANES), nch:] = xi
        return xr[SUBLANES - 1:SUBLANES, :], xi[SUBLANES - 1:SUBLANES, :]

    cr, ci = lax.fori_loop(0, rows // SUBLANES, body, (carry_ref[0:1, :], carry_ref[1:2, :]))
    carry_ref[0:1, :] = cr
    carry_ref[1:2, :] = ci
    y = _dot(xs_ref[...].astype(BF16), cd_ref[0]) + d_ref[...] * u
    o_ref[...] = 0.5 * y * (1.0 + jnp.tanh(math.sqrt(2.0 / math.pi) * (y + 0.044715 * (y * y * y))))


def _s5_params(a_re, a_im, log_step, b_re, b_im, c_re, c_im):
    lam_re = jnp.minimum(a_re, -1e-4)
    lam_im = a_im
    step = jnp.exp(log_step)[:, None]
    mag = jnp.exp(lam_re * step)
    lb_re = mag * jnp.cos(lam_im * step)
    lb_im = mag * jnp.sin(lam_im * step)
    den = lam_re * lam_re + lam_im * lam_im
    f_re = ((lb_re - 1.0) * lam_re + lb_im * lam_im) / den
    f_im = (lb_im * lam_re - (lb_re - 1.0) * lam_im) / den
    bb_re = f_re[..., None] * b_re - f_im[..., None] * b_im
    bb_im = f_re[..., None] * b_im + f_im[..., None] * b_re
    nb = S5_GROUPS // S5_GB
    eye = jnp.eye(S5_GB, dtype=F32)

    def bdiag_in(m):
        mt = m.transpose(0, 2, 1).reshape(nb, S5_GB, S5_GROUP, S5_STATE)
        return jnp.einsum('bghp,gk->bghkp', mt, eye).reshape(nb, S5_GB * S5_GROUP, S5_CH)

    def bdiag_out(m):
        mt = m.transpose(0, 2, 1).reshape(nb, S5_GB, S5_STATE, S5_GROUP)
        return jnp.einsum('bgph,gk->bgpkh', mt, eye).reshape(nb, S5_CH, S5_GB * S5_GROUP)

    bb = jnp.concatenate([bdiag_in(bb_re), bdiag_in(bb_im)], axis=2).astype(BF16)
    cd = jnp.concatenate([bdiag_out(c_re), -bdiag_out(c_im)], axis=1).astype(BF16)
    pr, pi = [lb_re], [lb_im]
    for _ in range(SUBLANES - 1):
        pr, pi = pr + [pr[-1] * lb_re - pi[-1] * lb_im], pi + [pr[-1] * lb_im + pi[-1] * lb_re]
    pw_r = jnp.stack(pr, axis=0).reshape(SUBLANES, nb, S5_CH).transpose(1, 0, 2)
    pw_i = jnp.stack(pi, axis=0).reshape(SUBLANES, nb, S5_CH).transpose(1, 0, 2)
    rowi = jnp.arange(SUBLANES)[None, :, None]
    kinds = []
    for s in (1, 2, 4):
        kinds.append(jnp.where(rowi >= s, pw_r[:, s - 1:s, :], 0.0))
        kinds.append(jnp.where(rowi >= s, pw_i[:, s - 1:s, :], 0.0))
    kinds += [pw_r, pw_i]
    tab = jnp.stack(kinds, axis=1)
    return bb, cd, tab


def _s5(u, a_re, a_im, log_step, b_re, b_im, c_re, c_im, d_skip, rows=1024):
    t = u.shape[0]
    rows = min(rows, t)
    nb = S5_GROUPS // S5_GB
    bb, cd, tab = _s5_params(a_re, a_im, log_step, b_re, b_im, c_re, c_im)
    return pl.pallas_call(
        _s5_kernel,
        grid=(nb, t // rows),
        in_specs=[pl.BlockSpec((rows, LANES), lambda b, i: (i, b)),
                  pl.BlockSpec((1, LANES, 2 * S5_CH), lambda b, i: (b, 0, 0)),
                  pl.BlockSpec((1, 2 * S5_CH, LANES), lambda b, i: (b, 0, 0)),
                  pl.BlockSpec((1, 8, SUBLANES, S5_CH), lambda b, i: (b, 0, 0, 0)),
                  pl.BlockSpec((1, LANES), lambda b, i: (0, b))],
        out_specs=pl.BlockSpec((rows, LANES), lambda b, i: (i, b)),
        out_shape=jax.ShapeDtypeStruct((t, S5_WIDTH), F32),
        scratch_shapes=[pltpu.VMEM((rows, 2 * S5_CH), F32), pltpu.VMEM((SUBLANES, S5_CH), F32)],
        compiler_params=_cparams("parallel", "arbitrary"),
        name="s5_ssm",
    )(u, bb, cd, tab, d_skip.reshape(1, S5_WIDTH))


def _router_kernel(h_ref, nw_ref, wr_ref, br_ref, x_ref, idx_ref, wt_ref):
    h = h_ref[...]
    ms = jnp.mean(h * h, axis=-1, keepdims=True)
    x = h * lax.rsqrt(ms + NORM_EPS) * nw_ref[...]
    x_ref[...] = x
    xh, xm = _split2(x)
    logits = (_dot(xh, wr_ref[0]) + (_dot(xm, wr_ref[0]) + _dot(xh, wr_ref[1]))) + br_ref[...]
    lane = lax.broadcasted_iota(jnp.int32, logits.shape, 1)
    big = jnp.int32(4 * LANES)
    ng, epg = N_EXPERT_GROUPS, EXPERTS_PER_GROUP
    lg = jnp.where(lane < ng, logits, NEG_BIG)
    mg = jnp.max(lg, axis=-1, keepdims=True)
    g_sel = jnp.min(jnp.where(lg == mg, lane, big), axis=-1, keepdims=True)
    pg_top = 1.0 / jnp.sum(jnp.exp(lg - mg), axis=-1, keepdims=True)
    lo = ng + g_sel * epg
    le = jnp.where(jnp.logical_and(lane >= lo, lane < lo + epg), logits, NEG_BIG)
    m1 = jnp.max(le, axis=-1, keepdims=True)
    i1 = jnp.min(jnp.where(le == m1, lane, big), axis=-1, keepdims=True)
    zsum = jnp.sum(jnp.exp(le - m1), axis=-1, keepdims=True)
    le2 = jnp.where(lane == i1, NEG_BIG, le)
    m2 = jnp.max(le2, axis=-1, keepdims=True)
    i2 = jnp.min(jnp.where(le2 == m2, lane, big), axis=-1, keepdims=True)
    p1 = 1.0 / zsum
    p2 = jnp.exp(m2 - m1) / zsum
    c1 = pg_top * p1 / (p1 + p2)
    c2 = pg_top * p2 / (p1 + p2)
    idx_ref[...] = jnp.where(lane == 0, i1 - ng, jnp.where(lane == 1, i2 - ng, 0))
    wt_ref[...] = jnp.where(lane == 0, c1, jnp.where(lane == 1, c2, 0.0))


def _router(h, norm_w, wg, bg, we, be, rows=256):
    t, d = h.shape
    rows = min(rows, t)
    ncol = N_EXPERT_GROUPS + N_EXPERTS
    wr = jnp.pad(jnp.concatenate([wg, we], axis=1), ((0, 0), (0, LANES - ncol)))
    wr2 = jnp.stack(_split2(wr))
    br = jnp.pad(jnp.concatenate([bg, be]), (0, LANES - ncol)).reshape(1, LANES)
    return pl.pallas_call(
        _router_kernel,
        grid=(t // rows,),
        in_specs=[pl.BlockSpec((rows, d), lambda i: (i, 0)),
                  pl.BlockSpec((1, d), lambda i: (0, 0)),
                  pl.BlockSpec((2, d, LANES), lambda i: (0, 0, 0)),
                  pl.BlockSpec((1, LANES), lambda i: (0, 0))],
        out_specs=[pl.BlockSpec((rows, d), lambda i: (i, 0)),
                   pl.BlockSpec((rows, LANES), lambda i: (i, 0)),
                   pl.BlockSpec((rows, LANES), lambda i: (i, 0))],
        out_shape=[jax.ShapeDtypeStruct((t, d), F32),
                   jax.ShapeDtypeStruct((t, LANES), jnp.int32),
                   jax.ShapeDtypeStruct((t, LANES), F32)],
        compiler_params=_cparams("parallel"),
        name="moe_router",
    )(h, norm_w.reshape(1, d), wr2, br)


def _row_copy(src_hbm, dst_ref, sem, src_row, dst_row):
    return pltpu.make_async_copy(src_hbm.at[pl.ds(src_row, 1)], dst_ref.at[pl.ds(dst_row, 1)], sem)


DMA_ISSUE_UNROLL = 8


def _gather_rows_kernel(tok_ref, nv_ref, x_hbm, o_ref, buf_ref, sem, *, rows):
    b = pl.program_id(0)
    nb = pl.num_programs(0)
    slot = b % 2
    nxt = jnp.minimum(b + 1, nb - 1)

    def issue(blk, to_slot):
        def body(i, _):
            _row_copy(x_hbm, buf_ref.at[to_slot], sem.at[to_slot], tok_ref[blk * rows + i], i).start()
            return 0
        lax.fori_loop(0, rows, body, 0, unroll=DMA_ISSUE_UNROLL)

    @pl.when(jnp.logical_and(b == 0, nv_ref[0] > 0))
    def _():
        issue(0, 0)

    @pl.when(jnp.logical_and(b + 1 < nb, nv_ref[nxt] > 0))
    def _():
        issue(b + 1, 1 - slot)

    @pl.when(nv_ref[b] > 0)
    def _():
        pltpu.make_async_copy(x_hbm.at[pl.ds(0, rows)], buf_ref.at[slot], sem.at[slot]).wait()
        o_ref[...] = buf_ref[slot].astype(o_ref.dtype)

    @pl.when(nv_ref[b] == 0)
    def _():
        o_ref[...] = jnp.zeros_like(o_ref)


def _gather_rows(x, tok, nvalid, rows):
    n = tok.shape[0]
    d = x.shape[1]
    return pl.pallas_call(
        functools.partial(_gather_rows_kernel, rows=rows),
        grid_spec=pltpu.PrefetchScalarGridSpec(
            num_scalar_prefetch=2,
            grid=(n // rows,),
            in_specs=[pl.BlockSpec(memory_space=pl.ANY)],
            out_specs=pl.BlockSpec((rows, d), lambda b, tok, nv: (b, 0)),
            scratch_shapes=[pltpu.VMEM((2, rows, d), x.dtype), pltpu.SemaphoreType.DMA((2,))],
        ),
        out_shape=jax.ShapeDtypeStruct((n, d), BF16),
        compiler_params=_cparams("arbitrary"),
        name="moe_gather",
    )(tok, nvalid, x)


def _run_on_valid_rows(nvalid, out_ref, compute):
    rows = out_ref.shape[0]
    half = rows // 2

    @pl.when(nvalid > half)
    def _():
        compute(rows)

    @pl.when(jnp.logical_and(nvalid > 0, nvalid <= half))
    def _():
        compute(half)
        out_ref[pl.ds(half, rows - half), :] = jnp.zeros((rows - half, out_ref.shape[1]), out_ref.dtype)

    @pl.when(nvalid == 0)
    def _():
        out_ref[...] = jnp.zeros_like(out_ref)


def _expert_up_kernel(be_ref, nv_ref, chg_ref, x_ref, w1_ref, w3_ref, h_ref, w1b_ref, w3b_ref):
    b = pl.program_id(1)

    @pl.when(chg_ref[b] == 1)
    def _():
        w1b_ref[...] = w1_ref[0, 0].astype(BF16)
        w3b_ref[...] = w3_ref[0, 0].astype(BF16)

    def compute(nrows):
        x = x_ref[pl.ds(0, nrows), :]
        h1 = _dot(x, w1b_ref[...])
        h3 = _dot(x, w3b_ref[...])
        h_ref[pl.ds(0, nrows), :] = (h1 * _sigmoid(h1) * h3).astype(h_ref.dtype)

    _run_on_valid_rows(nv_ref[b], h_ref, compute)


def _expert_down_kernel(be_ref, nv_ref, chg_ref, h_ref, w2_ref, o_ref, w2b_ref):
    b = pl.program_id(1)

    @pl.when(chg_ref[b] == 1)
    def _():
        w2b_ref[...] = w2_ref[0, 0].astype(BF16)

    def compute(nrows):
        o_ref[pl.ds(0, nrows), :] = _dot(h_ref[pl.ds(0, nrows), :], w2b_ref[...])

    _run_on_valid_rows(nv_ref[b], o_ref, compute)


def _experts(xg, block_e, nvalid, changed, w1, w3, w2, layer, ff_tile=512, out_tile=4096):
    n, d = xg.shape
    rows = MOE_ROWS
    ff = w1.shape[3]
    ff_tile = min(ff_tile, ff)
    out_tile = min(out_tile, d)
    nb = n // rows
    hbuf = pl.pallas_call(
        _expert_up_kernel,
        grid_spec=pltpu.PrefetchScalarGridSpec(
            num_scalar_prefetch=3,
            grid=(ff // ff_tile, nb),
            in_specs=[pl.BlockSpec((rows, d), lambda f, b, be, nv, ch: (b, 0)),
                      pl.BlockSpec((1, 1, d, ff_tile), lambda f, b, be, nv, ch: (layer, be[b], 0, f)),
                      pl.BlockSpec((1, 1, d, ff_tile), lambda f, b, be, nv, ch: (layer, be[b], 0, f))],
            out_specs=pl.BlockSpec((rows, ff_tile), lambda f, b, be, nv, ch: (b, f)),
            scratch_shapes=[pltpu.VMEM((d, ff_tile), BF16), pltpu.VMEM((d, ff_tile), BF16)],
        ),
        out_shape=jax.ShapeDtypeStruct((n, ff), BF16),
        compiler_params=_cparams("arbitrary", "arbitrary"),
        name="moe_expert_up",
    )(block_e, nvalid, changed, xg, w1, w3)
    return pl.pallas_call(
        _expert_down_kernel,
        grid_spec=pltpu.PrefetchScalarGridSpec(
            num_scalar_prefetch=3,
            grid=(d // out_tile, nb),
            in_specs=[pl.BlockSpec((rows, ff), lambda j, b, be, nv, ch: (b, 0)),
                      pl.BlockSpec((1, 1, ff, out_tile), lambda j, b, be, nv, ch: (layer, be[b], 0, j))],
            out_specs=pl.BlockSpec((rows, out_tile), lambda j, b, be, nv, ch: (b, j)),
            scratch_shapes=[pltpu.VMEM((ff, out_tile), BF16)],
        ),
        out_shape=jax.ShapeDtypeStruct((n, d), F32),
        compiler_params=_cparams("arbitrary", "arbitrary"),
        name="moe_expert_down",
    )(block_e, nvalid, changed, hbuf, w2)


def _combine_kernel(dest_ref, h_ref, wt_ref, yb_hbm, o_ref, buf_ref, sem, *, rows):
    b = pl.program_id(0)
    nb = pl.num_programs(0)
    slot = b % 2

    def issue(blk, to_slot):
        def body(i, _):
            for j in range(TOP_K):
                _row_copy(yb_hbm, buf_ref.at[to_slot], sem.at[to_slot],
                          dest_ref[(blk * rows + i) * TOP_K + j], j * rows + i).start()
            return 0
        lax.fori_loop(0, rows, body, 0, unroll=DMA_ISSUE_UNROLL // TOP_K)

    @pl.when(b == 0)
    def _():
        issue(0, 0)

    @pl.when(b + 1 < nb)
    def _():
        issue(b + 1, 1 - slot)

    pltpu.make_async_copy(yb_hbm.at[pl.ds(0, TOP_K * rows)], buf_ref.at[slot], sem.at[slot]).wait()
    wt = wt_ref[...]
    o_ref[...] = (h_ref[...] + wt[:, 0:1] * buf_ref[slot, pl.ds(0, rows), :]
                  + wt[:, 1:2] * buf_ref[slot, pl.ds(rows, rows), :])


def _combine(h, wt, yb, dest, rows=128):
    t, d = h.shape
    rows = min(rows, t)
    return pl.pallas_call(
        functools.partial(_combine_kernel, rows=rows),
        grid_spec=pltpu.PrefetchScalarGridSpec(
            num_scalar_prefetch=1,
            grid=(t // rows,),
            in_specs=[pl.BlockSpec((rows, d), lambda i, dr: (i, 0)),
                      pl.BlockSpec((rows, LANES), lambda i, dr: (i, 0)),
                      pl.BlockSpec(memory_space=pl.ANY)],
            out_specs=pl.BlockSpec((rows, d), lambda i, dr: (i, 0)),
            scratch_shapes=[pltpu.VMEM((2, TOP_K * rows, d), F32), pltpu.SemaphoreType.DMA((2,))],
        ),
        out_shape=jax.ShapeDtypeStruct((t, d), F32),
        compiler_params=_cparams("arbitrary"),
        name="moe_combine",
    )(dest, h, wt, yb)


def _routing_tables(expert_idx, rows):
    t = expert_idx.shape[0]
    a = t * TOP_K
    e_n = N_EXPERTS
    i32 = jnp.int32
    flat_e = expert_idx.reshape(-1)
    onehot = (flat_e[:, None] == jnp.arange(e_n, dtype=i32)[None, :]).astype(i32)
    counts = jnp.sum(onehot, axis=0)
    padded = (counts + rows - 1) // rows * rows
    pad_end = jnp.cumsum(padded)
    pad_start = pad_end - padded
    dest = jnp.sum(onehot * (jnp.cumsum(onehot, axis=0) - 1 + pad_start[None, :]), axis=1).astype(i32)
    n_blocks = a // rows + e_n
    blk_start = jnp.arange(n_blocks, dtype=i32) * rows
    block_e = jnp.minimum(jnp.sum((pad_end[None, :] <= blk_start[:, None]).astype(i32), axis=1), e_n - 1)
    blk_onehot = (block_e[:, None] == jnp.arange(e_n, dtype=i32)[None, :]).astype(i32)
    blk_counts = jnp.sum(blk_onehot * counts[None, :], axis=1)
    blk_pad_start = jnp.sum(blk_onehot * pad_start[None, :], axis=1)
    nvalid = jnp.clip(blk_counts - (blk_start - blk_pad_start), 0, rows).astype(i32)
    changed = jnp.concatenate([jnp.ones((1,), i32), (block_e[1:] != block_e[:-1]).astype(i32)])
    n_slots = n_blocks * rows
    slot_tok = (jnp.arange(n_slots, dtype=i32) % t).at[dest].set(jnp.arange(a, dtype=i32) // TOP_K)
    return slot_tok, dest, block_e.astype(i32), nvalid, changed


def _moe(h, norm_w, wg, bg, we, be, w1, w3, w2, layer):
    x, idx, wt = _router(h, norm_w, wg, bg, we, be)
    slot_tok, dest, block_e, nvalid, changed = _routing_tables(idx[:, :TOP_K], MOE_ROWS)
    xg = _gather_rows(x, slot_tok, nvalid, MOE_ROWS)
    yb = _experts(xg, block_e, nvalid, changed, w1, w3, w2, layer)
    return _combine(h, wt, yb, dest)


def _even_layer(h, norm_w, w_in, q_norm_w, k_norm_w, conv_w, conv_b, dt_bias, a_log, d_skip, ssd_norm_w, w_out):
    xn = _rmsnorm(h, norm_w, BF16)
    w_in_t = jnp.swapaxes(w_in, 0, 1)
    proj = _matmul([xn], w_in_t, F32, n=EVEN_MAIN, name="even_in_proj", w_transposed=True)
    w_dt_t = jnp.pad(w_in_t[EVEN_MAIN:], ((0, LANES - SSD_HEADS), (0, 0)))
    dt_pad = _matmul([xn], w_dt_t, F32, n=LANES, name="even_dt_proj", w_transposed=True)
    y_att = _dilated_attention(proj, q_norm_w, k_norm_w)
    y_ssd = _mamba2(proj, dt_pad, conv_w, conv_b, dt_bias, a_log, d_skip, ssd_norm_w)
    return _matmul([y_att, y_ssd], w_out, F32, n=w_out.shape[1], epilogue="residual", extra=(h,),
                   name="even_out_proj")


def _odd_layer(h, norm_w, w_in, mu, w0, w2, a0, a2, g2, k_k, k_a, r_k, lnx_w, lnx_b,
               s5_a_re, s5_a_im, s5_log_step, s5_b_re, s5_b_im, s5_c_re, s5_c_im, s5_d,
               glu_w, glu_b, w_out):
    xn = _rmsnorm(h, norm_w, BF16)
    w_in_t = jnp.swapaxes(w_in, 0, 1)
    pc = _matmul([xn], w_in_t, F32, n=RWKV_PC, name="odd_in_proj_rwkv", w_transposed=True)
    u = _matmul([xn], w_in_t[RWKV_IN:], F32, n=S5_WIDTH, name="odd_in_proj_s5", w_transposed=True)
    r, k, v, lw, kk, a, g = _rwkv_prep(pc, mu, w0, w2, a0, a2, g2, k_k, k_a)
    y_c = _rwkv_scan(r, k, v, lw, kk, a, g, r_k.reshape(-1), lnx_w, lnx_b)
    y_s = _s5(u, s5_a_re, s5_a_im, s5_log_step, s5_b_re, s5_b_im, s5_c_re, s5_c_im, s5_d.reshape(-1))
    y_d = _matmul([y_s.astype(BF16)], glu_w, BF16, n=S5_WIDTH, epilogue="glu",
                  extra=(y_s, glu_b.reshape(1, -1)), name="s5_glu")
    return _matmul([y_c, y_d], w_out, F32, n=w_out.shape[1], epilogue="residual", extra=(h,),
                   name="odd_out_proj")


def kernel(x, norm_mix_w, norm_ffn_w, ev_w_in, ev_q_norm_w, ev_k_norm_w, ev_conv_w, ev_conv_b, ev_dt_bias, ev_a_log, ev_d_skip, ev_ssd_norm_w, ev_w_out, od_w_in, od_mu, od_w0, od_w2, od_a0, od_a2, od_g2, od_k_k, od_k_a, od_r_k, od_lnx_w, od_lnx_b, od_s5_a_re, od_s5_a_im, od_s5_log_step, od_s5_b_re, od_s5_b_im, od_s5_c_re, od_s5_c_im, od_s5_d, od_glu_w, od_glu_b, od_w_out, moe_wg, moe_bg, moe_we, moe_be, moe_w1, moe_w3, moe_w2):
    b, s, d = x.shape
    assert b == 1
    h = x.reshape(s, d)
    depth = norm_mix_w.shape[0]
    for layer in range(depth):
        i = layer // 2
        if layer % 2 == 0:
            h = _even_layer(h, norm_mix_w[layer], ev_w_in[i], ev_q_norm_w[i], ev_k_norm_w[i], ev_conv_w[i],
                            ev_conv_b[i], ev_dt_bias[i], ev_a_log[i], ev_d_skip[i], ev_ssd_norm_w[i], ev_w_out[i])
        else:
            h = _odd_layer(h, norm_mix_w[layer], od_w_in[i], od_mu[i], od_w0[i], od_w2[i], od_a0[i], od_a2[i],
                           od_g2[i], od_k_k[i], od_k_a[i], od_r_k[i], od_lnx_w[i], od_lnx_b[i], od_s5_a_re[i],
                           od_s5_a_im[i], od_s5_log_step[i], od_s5_b_re[i], od_s5_b_im[i], od_s5_c_re[i],
                           od_s5_c_im[i], od_s5_d[i], od_glu_w[i], od_glu_b[i], od_w_out[i])
        h = _moe(h, norm_ffn_w[layer], moe_wg[layer], moe_bg[layer], moe_we[layer], moe_be[layer],
                 moe_w1, moe_w3, moe_w2, layer)
    return h.reshape(b, s, d)
```
